```python
import math
import jax, jax.numpy as jnp
from jax import lax
import numpy as np

D_MODEL = 1024
BATCH = 8
SEQ = 8192
DEPTH = 4

N_MIXERS = 4
MIX_WIDTH = D_MODEL
GROUP_WIDTH = MIX_WIDTH // N_MIXERS
SC_HEAD_DIM = 64
SC_TAPS = 3
POOL_WINDOWS = (2, 4, 8, 16)
POOL_GROUP = GROUP_WIDTH // len(POOL_WINDOWS)
CF_TAPS = 31
SSM_CH = 16
SSM_GROUPS = GROUP_WIDTH // SSM_CH
SSM_STATE = 64
SSM_DT_MIN = 1e-3
SSM_DT_MAX = 1e-1
D_FF = -(-(8 * D_MODEL) // (3 * 256)) * 256
IN_SIZES = (GROUP_WIDTH, GROUP_WIDTH, GROUP_WIDTH, GROUP_WIDTH, GROUP_WIDTH, GROUP_WIDTH, GROUP_WIDTH)
IN_WIDTH = sum(IN_SIZES)
IN_SPLITS = tuple(int(v) for v in np.cumsum(IN_SIZES)[:-1])
ALPHA = (2 * DEPTH) ** 0.25
BETA = (8 * DEPTH) ** -0.25
LN_EPS = 1e-5

kernel_name = "hybrid_parallel_mixer_trunk"


def layer_norm(x, g, b):
    xf = x.astype(jnp.float32)
    mu = jnp.mean(xf, axis=-1, keepdims=True)
    var = jnp.mean(jnp.square(xf - mu), axis=-1, keepdims=True)
    y = (xf - mu) * lax.rsqrt(var + LN_EPS) * g.astype(jnp.float32) + b.astype(jnp.float32)
    return y.astype(x.dtype)


def causal_dwconv(x, w):
    k, ch = w.shape
    return lax.conv_general_dilated(
        x, w[:, None, :].astype(x.dtype), window_strides=(1,), padding=[(k - 1, 0)],
        dimension_numbers=('NWC', 'WIO', 'NWC'), feature_group_count=ch)


def pool_mixer(z, w_pool, scale):
    zf = z.astype(jnp.float32)
    s = z.shape[1]
    pos = jnp.arange(1, s + 1, dtype=jnp.float32)[None, :, None]
    outs = []
    for i, w in enumerate(POOL_WINDOWS):
        zg = zf[..., i * POOL_GROUP:(i + 1) * POOL_GROUP]
        cs = jnp.cumsum(zg, axis=1)
        lagged = jnp.pad(cs, ((0, 0), (w, 0), (0, 0)))[:, :s]
        mean = (cs - lagged) / jnp.minimum(pos, float(w))
        outs.append(jnp.einsum('bsc,cd->bsd', mean - zg, w_pool[i].astype(jnp.float32)))
    return (jnp.concatenate(outs, axis=-1) * scale.astype(jnp.float32)).astype(z.dtype)


def conformer_conv(z, dw_w, dw_b, ln_g, ln_b):
    val, gate = jnp.split(z, 2, axis=-1)
    h = val * jax.nn.sigmoid(gate)
    h = causal_dwconv(h, dw_w) + dw_b
    h = layer_norm(h, ln_g, ln_b)
    return jax.nn.silu(h)


def ssm_mixer(u, lam_re, lam_im, log_dt, b_re, b_im, c_re, c_im, d, w_glu, b_glu):
    bsz, s, _ = u.shape
    f32 = jnp.float32
    uf = u.astype(f32).reshape(bsz, s, SSM_GROUPS, SSM_CH)
    lam = lax.complex(lam_re.astype(f32), lam_im.astype(f32))
    dt = jnp.exp(log_dt.astype(f32))[:, None]
    lam_bar = jnp.exp(lam * dt)
    bmat = lax.complex(b_re.astype(f32), b_im.astype(f32))
    b_bar = ((lam_bar - 1.0) / lam)[..., None] * bmat
    bu = jnp.einsum('gph,bsgh->bsgp', b_bar, uf)
    a = jnp.broadcast_to(lam_bar, (1, s) + lam_bar.shape)

    def combine(e1, e2):
        a1, b1 = e1
        a2, b2 = e2
        return a1 * a2, a2 * b1 + b2

    _, states = lax.associative_scan(combine, (a, bu), axis=1)
    cmat = lax.complex(c_re.astype(f32), c_im.astype(f32))
    y = jnp.real(jnp.einsum('ghp,bsgp->bsgh', cmat, states)) + d.astype(f32).reshape(SSM_GROUPS, SSM_CH) * uf
    y = y.reshape(bsz, s, GROUP_WIDTH)
    yg = jax.nn.gelu(y)
    out = yg * jax.nn.sigmoid(yg @ w_glu.astype(f32) + b_glu.astype(f32))
    return out.astype(u.dtype)


def _fwd_setup_inputs(seed: int = 0) -> dict:
    key = jax.random.key(seed)
    ks = jax.random.split(key, 32)
    L = DEPTH
    f32 = jnp.float32

    def nrm(k, shape, scale):
        return jax.random.normal(k, shape, f32) * scale

    lam_im_base = jnp.pi * jnp.arange(SSM_STATE, dtype=f32)
    return {
        "x": nrm(ks[0], (BATCH, SEQ, D_MODEL), 1.0),
        "c": nrm(ks[1], (BATCH, D_MODEL), 1.0),
        "w_ada": nrm(ks[2], (L, D_MODEL, 6 * D_MODEL), 0.1 * D_MODEL ** -0.5),
        "b_ada": nrm(ks[3], (L, 6 * D_MODEL), 0.01),
        "w_in": nrm(ks[4], (L, D_MODEL, IN_WIDTH), D_MODEL ** -0.5),
        "b_in": nrm(ks[5], (L, IN_WIDTH), 0.01),
        "sc_w": nrm(ks[6], (L, SC_TAPS, GROUP_WIDTH), SC_TAPS ** -0.5),
        "pool_w": nrm(ks[7], (L, len(POOL_WINDOWS), POOL_GROUP, POOL_GROUP), POOL_GROUP ** -0.5),
        "pool_scale": 1.0 + nrm(ks[8], (L, GROUP_WIDTH), 0.02),
        "cf_dw_w": nrm(ks[9], (L, CF_TAPS, GROUP_WIDTH), CF_TAPS ** -0.5),
        "cf_dw_b": nrm(ks[10], (L, GROUP_WIDTH), 0.01),
        "cf_ln_g": 1.0 + nrm(ks[11], (L, GROUP_WIDTH), 0.02),
        "cf_ln_b": nrm(ks[12], (L, GROUP_WIDTH), 0.01),
        "ssm_lam_re": -0.5 + nrm(ks[13], (L, SSM_GROUPS, SSM_STATE), 0.01),
        "ssm_lam_im": lam_im_base + nrm(ks[14], (L, SSM_GROUPS, SSM_STATE), 0.01),
        "ssm_log_dt": jax.random.uniform(ks[15], (L, SSM_GROUPS), f32, math.log(SSM_DT_MIN), math.log(SSM_DT_MAX)),
        "ssm_b_re": nrm(ks[16], (L, SSM_GROUPS, SSM_STATE, SSM_CH), (2 * SSM_CH) ** -0.5),
        "ssm_b_im": nrm(ks[17], (L, SSM_GROUPS, SSM_STATE, SSM_CH), (2 * SSM_CH) ** -0.5),
        "ssm_c_re": nrm(ks[18], (L, SSM_GROUPS, SSM_CH, SSM_STATE), SSM_STATE ** -0.5),
        "ssm_c_im": nrm(ks[19], (L, SSM_GROUPS, SSM_CH, SSM_STATE), SSM_STATE ** -0.5),
        "ssm_d": nrm(ks[20], (L, GROUP_WIDTH), 1.0),
        "ssm_w_glu": nrm(ks[21], (L, GROUP_WIDTH, GROUP_WIDTH), GROUP_WIDTH ** -0.5),
        "ssm_b_glu": nrm(ks[22], (L, GROUP_WIDTH), 0.01),
        "w_o": nrm(ks[23], (L, MIX_WIDTH, D_MODEL), BETA * MIX_WIDTH ** -0.5),
        "ln1_g": 1.0 + nrm(ks[24], (L, D_MODEL), 0.02),
        "ln1_b": nrm(ks[25], (L, D_MODEL), 0.01),
        "w_gate": nrm(ks[26], (L, D_MODEL, D_FF), D_MODEL ** -0.5),
        "w_up": nrm(ks[27], (L, D_MODEL, D_FF), D_MODEL ** -0.5),
        "w_down": nrm(ks[28], (L, D_FF, D_MODEL), BETA * D_FF ** -0.5),
        "ln2_g": 1.0 + nrm(ks[29], (L, D_MODEL), 0.02),
        "ln2_b": nrm(ks[30], (L, D_MODEL), 0.01),
    }


def _fwd_reference(x, c, w_ada, b_ada, w_in, b_in, sc_w, pool_w, pool_scale, cf_dw_w, cf_dw_b, cf_ln_g, cf_ln_b,
              ssm_lam_re, ssm_lam_im, ssm_log_dt, ssm_b_re, ssm_b_im, ssm_c_re, ssm_c_im, ssm_d,
              ssm_w_glu, ssm_b_glu, w_o, ln1_g, ln1_b, w_gate, w_up, w_down, ln2_g, ln2_b):
    cond = jax.nn.silu(c)
    for l in range(DEPTH):
        mod = cond @ w_ada[l] + b_ada[l]
        sh1, sc1, g1, sh2, sc2, g2 = [m[:, None, :] for m in jnp.split(mod, 6, axis=-1)]

        h = x * (1 + sc1) + sh1
        z = h @ w_in[l] + b_in[l]
        z_h, z_b, z_c, z_p, z_cv, z_cg, z_s = jnp.split(z, IN_SPLITS, axis=-1)
        y_a = z_b * causal_dwconv(z_c * z_h, sc_w[l])
        y_b = pool_mixer(z_p, pool_w[l], pool_scale[l])
        y_c = conformer_conv(jnp.concatenate([z_cv, z_cg], axis=-1),
                             cf_dw_w[l], cf_dw_b[l], cf_ln_g[l], cf_ln_b[l])
        y_d = ssm_mixer(z_s, ssm_lam_re[l], ssm_lam_im[l], ssm_log_dt[l], ssm_b_re[l], ssm_b_im[l],
                        ssm_c_re[l], ssm_c_im[l], ssm_d[l], ssm_w_glu[l], ssm_b_glu[l])
        y = jnp.concatenate([y_a, y_b, y_c, y_d], axis=-1) @ w_o[l]
        x = layer_norm(ALPHA * x + (1 + g1) * y, ln1_g[l], ln1_b[l])

        h = x * (1 + sc2) + sh2
        f = (jax.nn.silu(h @ w_gate[l]) * (h @ w_up[l])) @ w_down[l]
        x = layer_norm(ALPHA * x + (1 + g2) * f, ln2_g[l], ln2_b[l])
    return x


import jax as _jax
import jax.numpy as _jnp

TWIN_FORMAT = 'train_step'
FWD_PARAMS = ['x', 'c', 'w_ada', 'b_ada', 'w_in', 'b_in', 'sc_w', 'pool_w', 'pool_scale', 'cf_dw_w', 'cf_dw_b', 'cf_ln_g', 'cf_ln_b', 'ssm_lam_re', 'ssm_lam_im', 'ssm_log_dt', 'ssm_b_re', 'ssm_b_im', 'ssm_c_re', 'ssm_c_im', 'ssm_d', 'ssm_w_glu', 'ssm_b_glu', 'w_o', 'ln1_g', 'ln1_b', 'w_gate', 'w_up', 'w_down', 'ln2_g', 'ln2_b']
TWIN_WEIGHTS = ['w_ada', 'b_ada', 'w_in', 'b_in', 'sc_w', 'pool_w', 'pool_scale', 'cf_dw_w', 'cf_dw_b', 'cf_ln_g', 'cf_ln_b', 'ssm_lam_re', 'ssm_lam_im', 'ssm_log_dt', 'ssm_b_re', 'ssm_b_im', 'ssm_c_re', 'ssm_c_im', 'ssm_d', 'ssm_w_glu', 'ssm_b_glu', 'w_o', 'ln1_g', 'ln1_b', 'w_gate', 'w_up', 'w_down', 'ln2_g', 'ln2_b']
TWIN_DIFF_INPUT = 'x'
TWIN_INPUTS = ['x', 'c', 'w_ada', 'b_ada', 'w_in', 'b_in', 'sc_w', 'pool_w', 'pool_scale', 'cf_dw_w', 'cf_dw_b', 'cf_ln_g', 'cf_ln_b', 'ssm_lam_re', 'ssm_lam_im', 'ssm_log_dt', 'ssm_b_re', 'ssm_b_im', 'ssm_c_re', 'ssm_c_im', 'ssm_d', 'ssm_w_glu', 'ssm_b_glu', 'w_o', 'ln1_g', 'ln1_b', 'w_gate', 'w_up', 'w_down', 'ln2_g', 'ln2_b', 'loss_target', 'm_w_ada', 'm_b_ada', 'm_w_in', 'm_b_in', 'm_sc_w', 'm_pool_w', 'm_pool_scale', 'm_cf_dw_w', 'm_cf_dw_b', 'm_cf_ln_g', 'm_cf_ln_b', 'm_ssm_lam_re', 'm_ssm_lam_im', 'm_ssm_log_dt', 'm_ssm_b_re', 'm_ssm_b_im', 'm_ssm_c_re', 'm_ssm_c_im', 'm_ssm_d', 'm_ssm_w_glu', 'm_ssm_b_glu', 'm_w_o', 'm_ln1_g', 'm_ln1_b', 'm_w_gate', 'm_w_up', 'm_w_down', 'm_ln2_g', 'm_ln2_b', 'v_w_ada', 'v_b_ada', 'v_w_in', 'v_b_in', 'v_sc_w', 'v_pool_w', 'v_pool_scale', 'v_cf_dw_w', 'v_cf_dw_b', 'v_cf_ln_g', 'v_cf_ln_b', 'v_ssm_lam_re', 'v_ssm_lam_im', 'v_ssm_log_dt', 'v_ssm_b_re', 'v_ssm_b_im', 'v_ssm_c_re', 'v_ssm_c_im', 'v_ssm_d', 'v_ssm_w_glu', 'v_ssm_b_glu', 'v_w_o', 'v_ln1_g', 'v_ln1_b', 'v_w_gate', 'v_w_up', 'v_w_down', 'v_ln2_g', 'v_ln2_b']
TWIN_OUTPUTS = ['loss', 'grad_x', 'grad_w_ada', 'grad_b_ada', 'grad_w_in', 'grad_b_in', 'grad_sc_w', 'grad_pool_w', 'grad_pool_scale', 'grad_cf_dw_w', 'grad_cf_dw_b', 'grad_cf_ln_g', 'grad_cf_ln_b', 'grad_ssm_lam_re', 'grad_ssm_lam_im', 'grad_ssm_log_dt', 'grad_ssm_b_re', 'grad_ssm_b_im', 'grad_ssm_c_re', 'grad_ssm_c_im', 'grad_ssm_d', 'grad_ssm_w_glu', 'grad_ssm_b_glu', 'grad_w_o', 'grad_ln1_g', 'grad_ln1_b', 'grad_w_gate', 'grad_w_up', 'grad_w_down', 'grad_ln2_g', 'grad_ln2_b', 'delta_w_ada', 'delta_b_ada', 'delta_w_in', 'delta_b_in', 'delta_sc_w', 'delta_pool_w', 'delta_pool_scale', 'delta_cf_dw_w', 'delta_cf_dw_b', 'delta_cf_ln_g', 'delta_cf_ln_b', 'delta_ssm_lam_re', 'delta_ssm_lam_im', 'delta_ssm_log_dt', 'delta_ssm_b_re', 'delta_ssm_b_im', 'delta_ssm_c_re', 'delta_ssm_c_im', 'delta_ssm_d', 'delta_ssm_w_glu', 'delta_ssm_b_glu', 'delta_w_o', 'delta_ln1_g', 'delta_ln1_b', 'delta_w_gate', 'delta_w_up', 'delta_w_down', 'delta_ln2_g', 'delta_ln2_b', 'new_m_w_ada', 'new_m_b_ada', 'new_m_w_in', 'new_m_b_in', 'new_m_sc_w', 'new_m_pool_w', 'new_m_pool_scale', 'new_m_cf_dw_w', 'new_m_cf_dw_b', 'new_m_cf_ln_g', 'new_m_cf_ln_b', 'new_m_ssm_lam_re', 'new_m_ssm_lam_im', 'new_m_ssm_log_dt', 'new_m_ssm_b_re', 'new_m_ssm_b_im', 'new_m_ssm_c_re', 'new_m_ssm_c_im', 'new_m_ssm_d', 'new_m_ssm_w_glu', 'new_m_ssm_b_glu', 'new_m_w_o', 'new_m_ln1_g', 'new_m_ln1_b', 'new_m_w_gate', 'new_m_w_up', 'new_m_w_down', 'new_m_ln2_g', 'new_m_ln2_b', 'new_v_w_ada', 'new_v_b_ada', 'new_v_w_in', 'new_v_b_in', 'new_v_sc_w', 'new_v_pool_w', 'new_v_pool_scale', 'new_v_cf_dw_w', 'new_v_cf_dw_b', 'new_v_cf_ln_g', 'new_v_cf_ln_b', 'new_v_ssm_lam_re', 'new_v_ssm_lam_im', 'new_v_ssm_log_dt', 'new_v_ssm_b_re', 'new_v_ssm_b_im', 'new_v_ssm_c_re', 'new_v_ssm_c_im', 'new_v_ssm_d', 'new_v_ssm_w_glu', 'new_v_ssm_b_glu', 'new_v_w_o', 'new_v_ln1_g', 'new_v_ln1_b', 'new_v_w_gate', 'new_v_w_up', 'new_v_w_down', 'new_v_ln2_g', 'new_v_ln2_b']
TWIN_LEAF_KINDS = {'loss': 'loss', 'grad_x': 'grad_x', 'grad_w_ada': 'grad_w', 'grad_b_ada': 'grad_w', 'grad_w_in': 'grad_w', 'grad_b_in': 'grad_w', 'grad_sc_w': 'grad_w', 'grad_pool_w': 'grad_w', 'grad_pool_scale': 'grad_w', 'grad_cf_dw_w': 'grad_w', 'grad_cf_dw_b': 'grad_w', 'grad_cf_ln_g': 'grad_w', 'grad_cf_ln_b': 'grad_w', 'grad_ssm_lam_re': 'grad_w', 'grad_ssm_lam_im': 'grad_w', 'grad_ssm_log_dt': 'grad_w', 'grad_ssm_b_re': 'grad_w', 'grad_ssm_b_im': 'grad_w', 'grad_ssm_c_re': 'grad_w', 'grad_ssm_c_im': 'grad_w', 'grad_ssm_d': 'grad_w', 'grad_ssm_w_glu': 'grad_w', 'grad_ssm_b_glu': 'grad_w', 'grad_w_o': 'grad_w', 'grad_ln1_g': 'grad_w', 'grad_ln1_b': 'grad_w', 'grad_w_gate': 'grad_w', 'grad_w_up': 'grad_w', 'grad_w_down': 'grad_w', 'grad_ln2_g': 'grad_w', 'grad_ln2_b': 'grad_w', 'delta_w_ada': 'delta_w', 'delta_b_ada': 'delta_w', 'delta_w_in': 'delta_w', 'delta_b_in': 'delta_w', 'delta_sc_w': 'delta_w', 'delta_pool_w': 'delta_w', 'delta_pool_scale': 'delta_w', 'delta_cf_dw_w': 'delta_w', 'delta_cf_dw_b': 'delta_w', 'delta_cf_ln_g': 'delta_w', 'delta_cf_ln_b': 'delta_w', 'delta_ssm_lam_re': 'delta_w', 'delta_ssm_lam_im': 'delta_w', 'delta_ssm_log_dt': 'delta_w', 'delta_ssm_b_re': 'delta_w', 'delta_ssm_b_im': 'delta_w', 'delta_ssm_c_re': 'delta_w', 'delta_ssm_c_im': 'delta_w', 'delta_ssm_d': 'delta_w', 'delta_ssm_w_glu': 'delta_w', 'delta_ssm_b_glu': 'delta_w', 'delta_w_o': 'delta_w', 'delta_ln1_g': 'delta_w', 'delta_ln1_b': 'delta_w', 'delta_w_gate': 'delta_w', 'delta_w_up': 'delta_w', 'delta_w_down': 'delta_w', 'delta_ln2_g': 'delta_w', 'delta_ln2_b': 'delta_w', 'new_m_w_ada': 'new_m', 'new_m_b_ada': 'new_m', 'new_m_w_in': 'new_m', 'new_m_b_in': 'new_m', 'new_m_sc_w': 'new_m', 'new_m_pool_w': 'new_m', 'new_m_pool_scale': 'new_m', 'new_m_cf_dw_w': 'new_m', 'new_m_cf_dw_b': 'new_m', 'new_m_cf_ln_g': 'new_m', 'new_m_cf_ln_b': 'new_m', 'new_m_ssm_lam_re': 'new_m', 'new_m_ssm_lam_im': 'new_m', 'new_m_ssm_log_dt': 'new_m', 'new_m_ssm_b_re': 'new_m', 'new_m_ssm_b_im': 'new_m', 'new_m_ssm_c_re': 'new_m', 'new_m_ssm_c_im': 'new_m', 'new_m_ssm_d': 'new_m', 'new_m_ssm_w_glu': 'new_m', 'new_m_ssm_b_glu': 'new_m', 'new_m_w_o': 'new_m', 'new_m_ln1_g': 'new_m', 'new_m_ln1_b': 'new_m', 'new_m_w_gate': 'new_m', 'new_m_w_up': 'new_m', 'new_m_w_down': 'new_m', 'new_m_ln2_g': 'new_m', 'new_m_ln2_b': 'new_m', 'new_v_w_ada': 'new_v', 'new_v_b_ada': 'new_v', 'new_v_w_in': 'new_v', 'new_v_b_in': 'new_v', 'new_v_sc_w': 'new_v', 'new_v_pool_w': 'new_v', 'new_v_pool_scale': 'new_v', 'new_v_cf_dw_w': 'new_v', 'new_v_cf_dw_b': 'new_v', 'new_v_cf_ln_g': 'new_v', 'new_v_cf_ln_b': 'new_v', 'new_v_ssm_lam_re': 'new_v', 'new_v_ssm_lam_im': 'new_v', 'new_v_ssm_log_dt': 'new_v', 'new_v_ssm_b_re': 'new_v', 'new_v_ssm_b_im': 'new_v', 'new_v_ssm_c_re': 'new_v', 'new_v_ssm_c_im': 'new_v', 'new_v_ssm_d': 'new_v', 'new_v_ssm_w_glu': 'new_v', 'new_v_ssm_b_glu': 'new_v', 'new_v_w_o': 'new_v', 'new_v_ln1_g': 'new_v', 'new_v_ln1_b': 'new_v', 'new_v_w_gate': 'new_v', 'new_v_w_up': 'new_v', 'new_v_w_down': 'new_v', 'new_v_ln2_g': 'new_v', 'new_v_ln2_b': 'new_v'}


def _forward(args):
    return _fwd_reference(*[args[k] for k in FWD_PARAMS])


def _output_shape():
    out = _jax.eval_shape(lambda: _forward(_fwd_setup_inputs(0)))
    return out.shape, out.dtype

N_MICROBATCH = 1
ADAM_LR = 0.001
ADAM_B1 = 0.9
ADAM_B2 = 0.999
ADAM_EPS = 1e-08
ADAM_WD = 0.01
ADAM_STEP = 10
PER_EXAMPLE_BATCH_AXIS = {'x': 0, 'c': 0, 'loss_target': 0}
SHARED_INPUTS = []
_WEIGHT_DTYPES = {'w_ada': _jnp.float32, 'b_ada': _jnp.float32, 'w_in': _jnp.float32, 'b_in': _jnp.float32, 'sc_w': _jnp.float32, 'pool_w': _jnp.float32, 'pool_scale': _jnp.float32, 'cf_dw_w': _jnp.float32, 'cf_dw_b': _jnp.float32, 'cf_ln_g': _jnp.float32, 'cf_ln_b': _jnp.float32, 'ssm_lam_re': _jnp.float32, 'ssm_lam_im': _jnp.float32, 'ssm_log_dt': _jnp.float32, 'ssm_b_re': _jnp.float32, 'ssm_b_im': _jnp.float32, 'ssm_c_re': _jnp.float32, 'ssm_c_im': _jnp.float32, 'ssm_d': _jnp.float32, 'ssm_w_glu': _jnp.float32, 'ssm_b_glu': _jnp.float32, 'w_o': _jnp.float32, 'ln1_g': _jnp.float32, 'ln1_b': _jnp.float32, 'w_gate': _jnp.float32, 'w_up': _jnp.float32, 'w_down': _jnp.float32, 'ln2_g': _jnp.float32, 'ln2_b': _jnp.float32}
MOMENT_SCALE = {'w_ada': 3.484166e-02, 'b_ada': 5.793282e-02, 'w_in': 5.140974e-02, 'b_in': 4.802931e-02, 'sc_w': 6.599906e-02, 'pool_w': 5.720648e-02, 'pool_scale': 5.927986e-02, 'cf_dw_w': 4.075398e-02, 'cf_dw_b': 9.016886e-02, 'cf_ln_g': 5.030374e-02, 'cf_ln_b': 4.886541e-02, 'ssm_lam_re': 2.514601e-03, 'ssm_lam_im': 2.370657e-03, 'ssm_log_dt': 1.165214e+00, 'ssm_b_re': 1.281365e-03, 'ssm_b_im': 1.275432e-03, 'ssm_c_re': 1.798637e-03, 'ssm_c_im': 1.808695e-03, 'ssm_d': 3.272510e-02, 'ssm_w_glu': 6.929209e-03, 'ssm_b_glu': 1.198661e-02, 'w_o': 1.182966e-01, 'ln1_g': 1.650248e+00, 'ln1_b': 3.935779e-01, 'w_gate': 2.427532e-02, 'w_up': 2.350826e-02, 'w_down': 9.266053e-02, 'ln2_g': 3.202503e+01, 'ln2_b': 1.455663e+00}


def _to_microbatches(a, axis):
    t = _jnp.moveaxis(a, axis, 0)
    t = t.reshape((N_MICROBATCH, t.shape[0] // N_MICROBATCH) + t.shape[1:])
    return _jnp.moveaxis(t, 1, axis + 1)


def setup_inputs(seed: int = 0) -> dict:
    inp = _fwd_setup_inputs(seed)
    key = _jax.random.fold_in(_jax.random.key(seed), 7919)
    shape, _ = _output_shape()
    out = dict(inp)
    out["loss_target"] = _jax.random.normal(_jax.random.fold_in(key, 0), shape, _jnp.float32)
    for i, name in enumerate(TWIN_WEIGHTS):
        w = inp[name].astype(_jnp.float32)
        if MOMENT_SCALE is None:
            s = _jnp.sqrt(_jnp.mean(_jnp.square(w)) + 1e-30)
        else:
            s = MOMENT_SCALE[name]
        km, kv = _jax.random.split(_jax.random.fold_in(key, i + 1))
        out[name] = w
        out["m_" + name] = s * _jax.random.normal(km, w.shape, _jnp.float32)
        out["v_" + name] = (s * s) * _jax.random.uniform(kv, w.shape, _jnp.float32, 0.5, 1.5)
    if N_MICROBATCH > 1:
        for name, axis in PER_EXAMPLE_BATCH_AXIS.items():
            out[name] = _to_microbatches(out[name], axis)
    return {'x': out['x'], 'c': out['c'], 'w_ada': out['w_ada'], 'b_ada': out['b_ada'], 'w_in': out['w_in'], 'b_in': out['b_in'], 'sc_w': out['sc_w'], 'pool_w': out['pool_w'], 'pool_scale': out['pool_scale'], 'cf_dw_w': out['cf_dw_w'], 'cf_dw_b': out['cf_dw_b'], 'cf_ln_g': out['cf_ln_g'], 'cf_ln_b': out['cf_ln_b'], 'ssm_lam_re': out['ssm_lam_re'], 'ssm_lam_im': out['ssm_lam_im'], 'ssm_log_dt': out['ssm_log_dt'], 'ssm_b_re': out['ssm_b_re'], 'ssm_b_im': out['ssm_b_im'], 'ssm_c_re': out['ssm_c_re'], 'ssm_c_im': out['ssm_c_im'], 'ssm_d': out['ssm_d'], 'ssm_w_glu': out['ssm_w_glu'], 'ssm_b_glu': out['ssm_b_glu'], 'w_o': out['w_o'], 'ln1_g': out['ln1_g'], 'ln1_b': out['ln1_b'], 'w_gate': out['w_gate'], 'w_up': out['w_up'], 'w_down': out['w_down'], 'ln2_g': out['ln2_g'], 'ln2_b': out['ln2_b'], 'loss_target': out['loss_target'], 'm_w_ada': out['m_w_ada'], 'm_b_ada': out['m_b_ada'], 'm_w_in': out['m_w_in'], 'm_b_in': out['m_b_in'], 'm_sc_w': out['m_sc_w'], 'm_pool_w': out['m_pool_w'], 'm_pool_scale': out['m_pool_scale'], 'm_cf_dw_w': out['m_cf_dw_w'], 'm_cf_dw_b': out['m_cf_dw_b'], 'm_cf_ln_g': out['m_cf_ln_g'], 'm_cf_ln_b': out['m_cf_ln_b'], 'm_ssm_lam_re': out['m_ssm_lam_re'], 'm_ssm_lam_im': out['m_ssm_lam_im'], 'm_ssm_log_dt': out['m_ssm_log_dt'], 'm_ssm_b_re': out['m_ssm_b_re'], 'm_ssm_b_im': out['m_ssm_b_im'], 'm_ssm_c_re': out['m_ssm_c_re'], 'm_ssm_c_im': out['m_ssm_c_im'], 'm_ssm_d': out['m_ssm_d'], 'm_ssm_w_glu': out['m_ssm_w_glu'], 'm_ssm_b_glu': out['m_ssm_b_glu'], 'm_w_o': out['m_w_o'], 'm_ln1_g': out['m_ln1_g'], 'm_ln1_b': out['m_ln1_b'], 'm_w_gate': out['m_w_gate'], 'm_w_up': out['m_w_up'], 'm_w_down': out['m_w_down'], 'm_ln2_g': out['m_ln2_g'], 'm_ln2_b': out['m_ln2_b'], 'v_w_ada': out['v_w_ada'], 'v_b_ada': out['v_b_ada'], 'v_w_in': out['v_w_in'], 'v_b_in': out['v_b_in'], 'v_sc_w': out['v_sc_w'], 'v_pool_w': out['v_pool_w'], 'v_pool_scale': out['v_pool_scale'], 'v_cf_dw_w': out['v_cf_dw_w'], 'v_cf_dw_b': out['v_cf_dw_b'], 'v_cf_ln_g': out['v_cf_ln_g'], 'v_cf_ln_b': out['v_cf_ln_b'], 'v_ssm_lam_re': out['v_ssm_lam_re'], 'v_ssm_lam_im': out['v_ssm_lam_im'], 'v_ssm_log_dt': out['v_ssm_log_dt'], 'v_ssm_b_re': out['v_ssm_b_re'], 'v_ssm_b_im': out['v_ssm_b_im'], 'v_ssm_c_re': out['v_ssm_c_re'], 'v_ssm_c_im': out['v_ssm_c_im'], 'v_ssm_d': out['v_ssm_d'], 'v_ssm_w_glu': out['v_ssm_w_glu'], 'v_ssm_b_glu': out['v_ssm_b_glu'], 'v_w_o': out['v_w_o'], 'v_ln1_g': out['v_ln1_g'], 'v_ln1_b': out['v_ln1_b'], 'v_w_gate': out['v_w_gate'], 'v_w_up': out['v_w_up'], 'v_w_down': out['v_w_down'], 'v_ln2_g': out['v_ln2_g'], 'v_ln2_b': out['v_ln2_b']}


def _loss(weights, diff, rest, loss_target):
    with _jax.named_scope("forward"):
        args = {**rest, TWIN_DIFF_INPUT: diff, **{k: w.astype(_WEIGHT_DTYPES[k]) for k, w in weights.items()}}
        y = _forward(args)
    with _jax.named_scope("loss_head"):
        err = _jnp.square(y.astype(_jnp.float32) - loss_target)
        return 0.5 * _jnp.sum(_jnp.mean(err, axis=-1)) if err.ndim else 0.5 * err


def _adamw(w, g, m, v):
    m = ADAM_B1 * m + (1.0 - ADAM_B1) * g
    v = ADAM_B2 * v + (1.0 - ADAM_B2) * _jnp.square(g)
    m_hat = m / (1.0 - ADAM_B1 ** ADAM_STEP)
    v_hat = v / (1.0 - ADAM_B2 ** ADAM_STEP)
    delta = -ADAM_LR * (m_hat / (_jnp.sqrt(v_hat) + ADAM_EPS) + ADAM_WD * w)
    return delta, m, v


def reference(x, c, w_ada, b_ada, w_in, b_in, sc_w, pool_w, pool_scale, cf_dw_w, cf_dw_b, cf_ln_g, cf_ln_b, ssm_lam_re, ssm_lam_im, ssm_log_dt, ssm_b_re, ssm_b_im, ssm_c_re, ssm_c_im, ssm_d, ssm_w_glu, ssm_b_glu, w_o, ln1_g, ln1_b, w_gate, w_up, w_down, ln2_g, ln2_b, loss_target, m_w_ada, m_b_ada, m_w_in, m_b_in, m_sc_w, m_pool_w, m_pool_scale, m_cf_dw_w, m_cf_dw_b, m_cf_ln_g, m_cf_ln_b, m_ssm_lam_re, m_ssm_lam_im, m_ssm_log_dt, m_ssm_b_re, m_ssm_b_im, m_ssm_c_re, m_ssm_c_im, m_ssm_d, m_ssm_w_glu, m_ssm_b_glu, m_w_o, m_ln1_g, m_ln1_b, m_w_gate, m_w_up, m_w_down, m_ln2_g, m_ln2_b, v_w_ada, v_b_ada, v_w_in, v_b_in, v_sc_w, v_pool_w, v_pool_scale, v_cf_dw_w, v_cf_dw_b, v_cf_ln_g, v_cf_ln_b, v_ssm_lam_re, v_ssm_lam_im, v_ssm_log_dt, v_ssm_b_re, v_ssm_b_im, v_ssm_c_re, v_ssm_c_im, v_ssm_d, v_ssm_w_glu, v_ssm_b_glu, v_w_o, v_ln1_g, v_ln1_b, v_w_gate, v_w_up, v_w_down, v_ln2_g, v_ln2_b):
    given = dict(x=x, c=c, w_ada=w_ada, b_ada=b_ada, w_in=w_in, b_in=b_in, sc_w=sc_w, pool_w=pool_w, pool_scale=pool_scale, cf_dw_w=cf_dw_w, cf_dw_b=cf_dw_b, cf_ln_g=cf_ln_g, cf_ln_b=cf_ln_b, ssm_lam_re=ssm_lam_re, ssm_lam_im=ssm_lam_im, ssm_log_dt=ssm_log_dt, ssm_b_re=ssm_b_re, ssm_b_im=ssm_b_im, ssm_c_re=ssm_c_re, ssm_c_im=ssm_c_im, ssm_d=ssm_d, ssm_w_glu=ssm_w_glu, ssm_b_glu=ssm_b_glu, w_o=w_o, ln1_g=ln1_g, ln1_b=ln1_b, w_gate=w_gate, w_up=w_up, w_down=w_down, ln2_g=ln2_g, ln2_b=ln2_b, loss_target=loss_target, m_w_ada=m_w_ada, m_b_ada=m_b_ada, m_w_in=m_w_in, m_b_in=m_b_in, m_sc_w=m_sc_w, m_pool_w=m_pool_w, m_pool_scale=m_pool_scale, m_cf_dw_w=m_cf_dw_w, m_cf_dw_b=m_cf_dw_b, m_cf_ln_g=m_cf_ln_g, m_cf_ln_b=m_cf_ln_b, m_ssm_lam_re=m_ssm_lam_re, m_ssm_lam_im=m_ssm_lam_im, m_ssm_log_dt=m_ssm_log_dt, m_ssm_b_re=m_ssm_b_re, m_ssm_b_im=m_ssm_b_im, m_ssm_c_re=m_ssm_c_re, m_ssm_c_im=m_ssm_c_im, m_ssm_d=m_ssm_d, m_ssm_w_glu=m_ssm_w_glu, m_ssm_b_glu=m_ssm_b_glu, m_w_o=m_w_o, m_ln1_g=m_ln1_g, m_ln1_b=m_ln1_b, m_w_gate=m_w_gate, m_w_up=m_w_up, m_w_down=m_w_down, m_ln2_g=m_ln2_g, m_ln2_b=m_ln2_b, v_w_ada=v_w_ada, v_b_ada=v_b_ada, v_w_in=v_w_in, v_b_in=v_b_in, v_sc_w=v_sc_w, v_pool_w=v_pool_w, v_pool_scale=v_pool_scale, v_cf_dw_w=v_cf_dw_w, v_cf_dw_b=v_cf_dw_b, v_cf_ln_g=v_cf_ln_g, v_cf_ln_b=v_cf_ln_b, v_ssm_lam_re=v_ssm_lam_re, v_ssm_lam_im=v_ssm_lam_im, v_ssm_log_dt=v_ssm_log_dt, v_ssm_b_re=v_ssm_b_re, v_ssm_b_im=v_ssm_b_im, v_ssm_c_re=v_ssm_c_re, v_ssm_c_im=v_ssm_c_im, v_ssm_d=v_ssm_d, v_ssm_w_glu=v_ssm_w_glu, v_ssm_b_glu=v_ssm_b_glu, v_w_o=v_w_o, v_ln1_g=v_ln1_g, v_ln1_b=v_ln1_b, v_w_gate=v_w_gate, v_w_up=v_w_up, v_w_down=v_w_down, v_ln2_g=v_ln2_g, v_ln2_b=v_ln2_b)
    weights = {n: given[n] for n in TWIN_WEIGHTS}
    shared = {n: given[n] for n in SHARED_INPUTS}
    per_example = {n: given[n] for n in ['x', 'c']}
    grad_fn = _jax.value_and_grad(_loss, argnums=(0, 1))

    def one_microbatch(ex, loss_target):
        ex = dict(ex)
        diff = ex.pop(TWIN_DIFF_INPUT)
        return grad_fn(weights, diff, {**shared, **ex}, loss_target)

    if N_MICROBATCH == 1:
        loss, (grad_w, grad_x) = one_microbatch(per_example, given["loss_target"])
    else:
        def body(carry, xs):
            loss_sum, grad_sum = carry
            l_k, (gw_k, gx_k) = one_microbatch(xs[0], xs[1])
            with _jax.named_scope("update"):
                return (loss_sum + l_k, _jax.tree.map(_jnp.add, grad_sum, gw_k)), gx_k

        init = (_jnp.zeros((), _jnp.float32), _jax.tree.map(_jnp.zeros_like, weights))
        (loss, grad_w), grad_x = _jax.lax.scan(body, init, (per_example, given["loss_target"]))
    with _jax.named_scope("update"):
        delta_w, new_m, new_v = {}, {}, {}
        for n in TWIN_WEIGHTS:
            delta_w[n], new_m[n], new_v[n] = _adamw(weights[n], grad_w[n], given["m_" + n], given["v_" + n])
    return (loss, grad_x, *[grad_w[n] for n in TWIN_WEIGHTS], *[delta_w[n] for n in TWIN_WEIGHTS],
            *[new_m[n] for n in TWIN_WEIGHTS], *[new_v[n] for n in TWIN_WEIGHTS])
```

```python
import functools
import math

import jax
import jax.numpy as jnp
import numpy as np
from jax import lax
from jax.experimental import pallas as pl
from jax.experimental.pallas import tpu as pltpu

F32 = jnp.float32
BF16 = jnp.bfloat16
SDS = jax.ShapeDtypeStruct

D = 1024
DEPTH = 4
GW = 256
IN_W = 7 * GW
D_FF = 2816
SC_TAPS = 3
POOL_WINDOWS = (2, 4, 8, 16)
CF_TAPS = 31
SSM_G, SSM_H, SSM_P = 16, 16, 64
NST = SSM_G * SSM_P
ALPHA = (2 * DEPTH) ** 0.25
LN_EPS = 1e-5
ADAM_LR, ADAM_B1, ADAM_B2, ADAM_EPS, ADAM_WD, ADAM_STEP = 0.001, 0.9, 0.999, 1e-08, 0.01, 10

TR = 512
HALO = 32
FF_BLK = 1408
VMEM_LIMIT = 56 * 1024 * 1024
N_DEV = 8


def _cp(n_axes):
    return pltpu.CompilerParams(dimension_semantics=("arbitrary",) * n_axes, vmem_limit_bytes=VMEM_LIMIT)


def _sigmoid(x):
    return 1.0 / (1.0 + jnp.exp(-x))


_GELU_C = math.sqrt(2.0 / math.pi)


def _gelu(x):
    t = jnp.tanh(_GELU_C * (x + 0.044715 * (x * x * x)))
    return 0.5 * x * (1.0 + t), t


def _gelu_grad(x, t):
    return 0.5 * (1.0 + t) + 0.5 * x * (1.0 - t * t) * (_GELU_C * (1.0 + 3 * 0.044715 * (x * x)))


def _ln_stats(r):
    mu = jnp.mean(r, axis=-1, keepdims=True)
    xc = r - mu
    var = jnp.mean(xc * xc, axis=-1, keepdims=True)
    rstd = lax.rsqrt(var + LN_EPS)
    return xc * rstd, rstd


def _ln_bwd(dy, r, gamma):
    xhat, rstd = _ln_stats(r)
    dxh = dy * gamma
    m1 = jnp.mean(dxh, axis=-1, keepdims=True)
    m2 = jnp.mean(dxh * xhat, axis=-1, keepdims=True)
    return rstd * (dxh - m1 - xhat * m2), dy * xhat


def _rowsum(v):
    return jnp.sum(v, axis=0, keepdims=True)


def _dot(a, b):
    return jnp.dot(a, b, preferred_element_type=F32)


def _dot_nt(a, b):
    return lax.dot_general(a, b, (((1,), (1,)), ((), ())), preferred_element_type=F32)


def _dot_tn(a, b):
    return lax.dot_general(a, b, (((0,), (0,)), ((), ())), preferred_element_type=F32)


def _full(shape):
    return pl.BlockSpec(shape, lambda *_: (0,) * len(shape))


def mm_in_fwd(x, mod, w, b):
    s = x.shape[0]

    def body(x_ref, mod_ref, w_ref, b_ref, z_ref, h_ref):
        h = x_ref[...] * (1.0 + mod_ref[1:2, :]) + mod_ref[0:1, :]
        hb = h.astype(BF16)
        h_ref[...] = hb
        z_ref[...] = _dot(hb, w_ref[...]) + b_ref[...]

    return pl.pallas_call(
        body, name="mm_in_fwd", grid=(s // TR,),
        in_specs=[pl.BlockSpec((TR, D), lambda i: (i, 0)), _full((8, D)), _full((D, IN_W)), _full((1, IN_W))],
        out_specs=[pl.BlockSpec((TR, IN_W), lambda i: (i, 0)), pl.BlockSpec((TR, D), lambda i: (i, 0))],
        out_shape=[SDS((s, IN_W), F32), SDS((s, D), BF16)], compiler_params=_cp(1),
    )(x, mod, w, b)


def _pool_consts(i, rows):
    lane = lax.broadcasted_iota(jnp.int32, (1, GW), 1) // (GW // 4)
    wl = jnp.where(lane == 0, 2.0, jnp.where(lane == 1, 4.0, jnp.where(lane == 2, 8.0, 16.0))).astype(F32)
    pos = (i * TR + 1 + lax.broadcasted_iota(jnp.int32, (rows, 1), 0)).astype(F32)
    return lane, jnp.minimum(pos, wl)


def _pick_window(lane, c2, c4, c8, c16):
    return jnp.where(lane == 0, c2, jnp.where(lane == 1, c4, jnp.where(lane == 2, c8, c16)))


def mix_fwd(z, scw, pblk, pscale, cw, cvec):
    s = z.shape[0]
    nt = s // TR
    r = TR // HALO

    def body(z_ref, zp_ref, scw_ref, pblk_ref, ps_ref, cw_ref, cvec_ref, y_ref, hc_ref, buf):
        i = pl.program_id(0)
        z = z_ref[...]
        zp = jnp.where(i > 0, zp_ref[...], 0.0)
        buf[0:HALO, 0:256] = zp[:, 512:768] * zp[:, 0:256]
        buf[HALO:HALO + TR, 0:256] = z[:, 512:768] * z[:, 0:256]
        buf[0:HALO, 256:512] = zp[:, 768:1024]
        buf[HALO:HALO + TR, 256:512] = z[:, 768:1024]
        buf[0:HALO, 512:768] = zp[:, 1024:1280] * _sigmoid(zp[:, 1280:1536])
        buf[HALO:HALO + TR, 512:768] = z[:, 1024:1280] * _sigmoid(z[:, 1280:1536])
        cv = scw_ref[0:1, :] * buf[pl.ds(HALO - 2, TR), 0:256]
        for j in range(1, SC_TAPS):
            cv = cv + scw_ref[j:j + 1, :] * buf[pl.ds(HALO - 2 + j, TR), 0:256]
        y_ref[:, 0:256] = (z[:, 256:512] * cv).astype(BF16)
        lane, cnt = _pool_consts(i, TR)
        acc = buf[pl.ds(HALO, TR), 256:512]
        caps = {}
        for k in range(1, 16):
            acc = acc + buf[pl.ds(HALO - k, TR), 256:512]
            if k + 1 in POOL_WINDOWS:
                caps[k + 1] = acc
        dmean = _pick_window(lane, caps[2], caps[4], caps[8], caps[16]) / cnt - z[:, 768:1024]
        y_ref[:, 256:512] = (_dot(dmean.astype(BF16), pblk_ref[...]) * ps_ref[...]).astype(BF16)
        hc = cw_ref[0:1, :] * buf[pl.ds(HALO - 30, TR), 512:768]
        for j in range(1, CF_TAPS):
            hc = hc + cw_ref[j:j + 1, :] * buf[pl.ds(HALO - 30 + j, TR), 512:768]
        hc = hc + cvec_ref[0:1, :]
        hc_ref[...] = hc
        xhat, _ = _ln_stats(hc)
        hn = xhat * cvec_ref[1:2, :] + cvec_ref[2:3, :]
        y_ref[:, 512:768] = (hn * _sigmoid(hn)).astype(BF16)

    return pl.pallas_call(
        body, name="mix_fwd", grid=(nt,),
        in_specs=[pl.BlockSpec((TR, 1536), lambda i: (i, 0)),
                  pl.BlockSpec((HALO, 1536), lambda i: (jnp.maximum(i * r - 1, 0), 0)),
                  _full((8, GW)), _full((GW, GW)), _full((1, GW)), _full((32, GW)), _full((8, GW))],
        out_specs=[pl.BlockSpec((TR, 768), lambda i: (i, 0)), pl.BlockSpec((TR, GW), lambda i: (i, 0))],
        out_shape=[SDS((s, D), BF16), SDS((s, GW), F32)],
        scratch_shapes=[pltpu.VMEM((HALO + TR, 768), F32)], compiler_params=_cp(1),
    )(z, z, scw, pblk, pscale, cw, cvec)


def _scan_tables(pw_ref, j, conj, reverse):
    cr = slice(j * 128, (j + 1) * 128)
    ci = slice(NST + j * 128, NST + (j + 1) * 128)
    row = lax.broadcasted_iota(jnp.int32, (8, 128), 0)
    sgn = -1.0 if conj else 1.0
    steps = []
    for sft in (1, 2, 4):
        ar = jnp.broadcast_to(pw_ref[sft - 1:sft, cr], (8, 128))
        ai = jnp.broadcast_to(pw_ref[sft - 1:sft, ci], (8, 128)) * sgn
        keep = (row < 8 - sft) if reverse else (row >= sft)
        steps.append((sft, jnp.where(keep, ar, 0.0), jnp.where(keep, ai, 0.0)))
    pr = pw_ref[:, cr]
    pi = pw_ref[:, ci] * sgn
    if reverse:
        pr = jnp.concatenate([pr[7 - k:8 - k, :] for k in range(8)], axis=0)
        pi = jnp.concatenate([pi[7 - k:8 - k, :] for k in range(8)], axis=0)
    return steps, pr, pi


def _scan_rows(st, base, nrows, pw_ref, carry_ref, conj, reverse):
    ng = nrows // 8
    for j in range(NST // 128):
        cr = slice(j * 128, (j + 1) * 128)
        ci = slice(NST + j * 128, NST + (j + 1) * 128)
        steps, pr, pi = _scan_tables(pw_ref, j, conj, reverse)

        def group(gi, carry, cr=cr, ci=ci, steps=steps, pr=pr, pi=pi):
            c_r, c_i = carry
            g = (ng - 1 - gi) if reverse else gi
            r0 = pl.multiple_of(base + g * 8, 8)
            xr = st[pl.ds(r0, 8), cr]
            xi = st[pl.ds(r0, 8), ci]
            for sft, ar, ai in steps:
                amt = (8 - sft) if reverse else sft
                rr = pltpu.roll(xr, amt, 0)
                ri = pltpu.roll(xi, amt, 0)
                xr, xi = xr + (ar * rr - ai * ri), xi + (ar * ri + ai * rr)
            xr, xi = xr + (pr * c_r - pi * c_i), xi + (pr * c_i + pi * c_r)
            st[pl.ds(r0, 8), cr] = xr
            st[pl.ds(r0, 8), ci] = xi
            edge = slice(0, 1) if reverse else slice(7, 8)
            return jnp.broadcast_to(xr[edge, :], (8, 128)), jnp.broadcast_to(xi[edge, :], (8, 128))

        c_r, c_i = lax.fori_loop(0, ng, group, (carry_ref[:, cr], carry_ref[:, ci]))
        carry_ref[:, cr] = c_r
        carry_ref[:, ci] = c_i


def ssm_fwd(z, ymix, bblk, cblk, pw, dvec, wglu, bglu):
    s = z.shape[0]
    nt = s // TR

    def body(u_ref, ymix_in, bblk_ref, cblk_ref, pw_ref, dvec_ref, wglu_ref, bglu_ref, yd_ref, y_ref, xb_ref, st, carry):
        i = pl.program_id(0)

        @pl.when(i == 0)
        def _():
            carry[...] = jnp.zeros_like(carry)

        xb_ref[...] = carry[...]
        u = u_ref[...]
        st[...] = _dot(u.astype(BF16), bblk_ref[...])
        _scan_rows(st, 0, TR, pw_ref, carry, conj=False, reverse=False)
        y = _dot(st[...].astype(BF16), cblk_ref[...]) + dvec_ref[...] * u
        y_ref[...] = y
        yg, _ = _gelu(y)
        q = _dot(yg.astype(BF16), wglu_ref[...]) + bglu_ref[...]
        yd_ref[...] = (yg * _sigmoid(q)).astype(BF16)

    return pl.pallas_call(
        body, name="ssm_fwd", grid=(nt,),
        in_specs=[pl.BlockSpec((TR, GW), lambda i: (i, 6)), pl.BlockSpec(memory_space=pl.ANY),
                  _full((GW, 2 * NST)), _full((2 * NST, GW)), _full((8, 2 * NST)), _full((1, GW)),
                  _full((GW, GW)), _full((1, GW))],
        out_specs=[pl.BlockSpec((TR, GW), lambda i: (i, 3)), pl.BlockSpec((TR, GW), lambda i: (i, 0)),
                   pl.BlockSpec((8, 2 * NST), lambda i: (i, 0))],
        out_shape=[SDS((s, D), BF16), SDS((s, GW), F32), SDS((nt * 8, 2 * NST), F32)],
        scratch_shapes=[pltpu.VMEM((TR, 2 * NST), F32), pltpu.VMEM((8, 2 * NST), F32)],
        input_output_aliases={1: 0}, compiler_params=_cp(1),
    )(z, ymix, bblk, cblk, pw, dvec, wglu, bglu)


def mm_res_ln(a, w, xres, mod, lng, lnb, g_row, h2_rows):
    s, k = a.shape

    def body(a_ref, w_ref, x_ref, mod_ref, g_ref, b_ref, f_ref, r_ref, xo_ref, *h_ref):
        f = _dot(a_ref[...], w_ref[...])
        f_ref[...] = f
        r = ALPHA * x_ref[...] + (1.0 + mod_ref[g_row:g_row + 1, :]) * f
        r_ref[...] = r
        xhat, _ = _ln_stats(r)
        xo = xhat * g_ref[...] + b_ref[...]
        xo_ref[...] = xo
        if h2_rows is not None:
            sh, sc = h2_rows
            h_ref[0][...] = (xo * (1.0 + mod_ref[sc:sc + 1, :]) + mod_ref[sh:sh + 1, :]).astype(BF16)

    row = pl.BlockSpec((TR, D), lambda i: (i, 0))
    outs = [SDS((s, D), F32)] * 3 + ([SDS((s, D), BF16)] if h2_rows is not None else [])
    return pl.pallas_call(
        body, name="mm_res_ln_%d" % k, grid=(s // TR,),
        in_specs=[pl.BlockSpec((TR, k), lambda i: (i, 0)), _full((k, D)), row, _full((8, D)), _full((1, D)), _full((1, D))],
        out_specs=[row] * len(outs), out_shape=outs, compiler_params=_cp(1),
    )(a, w, xres, mod, lng, lnb)


def mm_gate_up(h2, wg, wu):
    s = h2.shape[0]
    nn = D_FF // FF_BLK

    def body(h_ref, wg_ref, wu_ref, gt_ref, up_ref, a_ref):
        h = h_ref[...]
        gt = _dot(h, wg_ref[...])
        up = _dot(h, wu_ref[...])
        gt_ref[...] = gt
        up_ref[...] = up
        a_ref[...] = (gt * _sigmoid(gt) * up).astype(BF16)

    wspec = pl.BlockSpec((D, FF_BLK), lambda n, i: (0, n))
    ospec = pl.BlockSpec((TR, FF_BLK), lambda n, i: (i, n))
    return pl.pallas_call(
        body, name="mm_gate_up", grid=(nn, s // TR),
        in_specs=[pl.BlockSpec((TR, D), lambda n, i: (i, 0)), wspec, wspec],
        out_specs=[ospec, ospec, ospec],
        out_shape=[SDS((s, D_FF), F32), SDS((s, D_FF), F32), SDS((s, D_FF), BF16)], compiler_params=_cp(2),
    )(h2, wg, wu)


def loss_fwd(x, target):
    s = x.shape[0]

    def body(x_ref, t_ref, l_ref, dx_ref):
        i = pl.program_id(0)

        @pl.when(i == 0)
        def _():
            l_ref[...] = jnp.zeros_like(l_ref)

        e = x_ref[...] - t_ref[...]
        dx_ref[...] = e * (1.0 / D)
        part = jnp.sum(jnp.mean(e * e, axis=-1, keepdims=True), axis=0, keepdims=True)
        l_ref[...] = l_ref[...] + 0.5 * part

    row = pl.BlockSpec((TR, D), lambda i: (i, 0))
    return pl.pallas_call(
        body, name="loss_fwd", grid=(s // TR,), in_specs=[row, row], out_specs=[_full((8, 128)), row],
        out_shape=[SDS((8, 128), F32), SDS((s, D), F32)], compiler_params=_cp(1),
    )(x, target)


def bwd_ln_o(dx, r, y, w_o, mod, lng):
    s = dx.shape[0]

    def body(dx_ref, r_ref, y_ref, w_ref, mod_ref, g_ref, dr_ref, dy_ref, dm_ref, st_ref):
        i = pl.program_id(0)

        @pl.when(i == 0)
        def _():
            st_ref[...] = jnp.zeros_like(st_ref)

        dx = dx_ref[...]
        dr, dg = _ln_bwd(dx, r_ref[...], g_ref[...])
        dr_ref[...] = dr
        dy = ((1.0 + mod_ref[2:3, :]) * dr).astype(BF16)
        dy_ref[...] = dy
        dm_ref[...] = _dot_nt(dy, w_ref[...])
        st_ref[0:1, :] += _rowsum(dg)
        st_ref[1:2, :] += _rowsum(dx)
        st_ref[2:3, :] += _rowsum(dr * y_ref[...])

    row = pl.BlockSpec((TR, D), lambda i: (i, 0))
    return pl.pallas_call(
        body, name="bwd_ln_o", grid=(s // TR,),
        in_specs=[row, row, row, _full((D, D)), _full((8, D)), _full((1, D))],
        out_specs=[row, row, row, _full((8, D))],
        out_shape=[SDS((s, D), F32), SDS((s, D), BF16), SDS((s, D), F32), SDS((8, D), F32)], compiler_params=_cp(1),
    )(dx, r, y, w_o, mod, lng)


def bwd_ln_down(dx, r, f, gt, up, w_down, mod, lng):
    s = dx.shape[0]
    nn = D_FF // FF_BLK

    def body(dx_ref, r_ref, f_ref, gt_ref, up_ref, w_ref, mod_ref, g_ref, dr_ref, df_ref, dgt_ref, dup_ref, st_ref):
        i = pl.program_id(0)
        n = pl.program_id(1)

        @pl.when((i == 0) & (n == 0))
        def _():
            st_ref[...] = jnp.zeros_like(st_ref)

        @pl.when(n == 0)
        def _():
            dx = dx_ref[...]
            dr, dg = _ln_bwd(dx, r_ref[...], g_ref[...])
            dr_ref[...] = dr
            df_ref[...] = ((1.0 + mod_ref[5:6, :]) * dr).astype(BF16)
            st_ref[0:1, :] += _rowsum(dg)
            st_ref[1:2, :] += _rowsum(dx)
            st_ref[2:3, :] += _rowsum(dr * f_ref[...])

        da = _dot_nt(df_ref[...], w_ref[...])
        gt = gt_ref[...]
        sg = _sigmoid(gt)
        dgt_ref[...] = (da * up_ref[...] * (sg * (1.0 + gt * (1.0 - sg)))).astype(BF16)
        dup_ref[...] = (da * (gt * sg)).astype(BF16)

    row = pl.BlockSpec((TR, D), lambda i, n: (i, 0))
    ff = pl.BlockSpec((TR, FF_BLK), lambda i, n: (i, n))
    return pl.pallas_call(
        body, name="bwd_ln_down", grid=(s // TR, nn),
        in_specs=[row, row, row, ff, ff, pl.BlockSpec((FF_BLK, D), lambda i, n: (n, 0)),
                  pl.BlockSpec((8, D), lambda i, n: (0, 0)), pl.BlockSpec((1, D), lambda i, n: (0, 0))],
        out_specs=[row, row, ff, ff, pl.BlockSpec((8, D), lambda i, n: (0, 0))],
        out_shape=[SDS((s, D), F32), SDS((s, D), BF16), SDS((s, D_FF), BF16), SDS((s, D_FF), BF16), SDS((8, D), F32)],
        compiler_params=_cp(2),
    )(dx, r, f, gt, up, w_down, mod, lng)


def bwd_dx_mod(parts, dr, xin, mod, sh_row, sc_row, name):
    s = dr.shape[0]
    npart = len(parts)

    def body(*refs):
        ins, (dx_ref, st_ref) = refs[:2 * npart + 3], refs[2 * npart + 3:]
        dr_ref, x_ref, mod_ref = ins[2 * npart:]
        i = pl.program_id(0)

        @pl.when(i == 0)
        def _():
            st_ref[...] = jnp.zeros_like(st_ref)

        dh = _dot_nt(ins[0][...], ins[1][...])
        for k in range(1, npart):
            dh = dh + _dot_nt(ins[2 * k][...], ins[2 * k + 1][...])
        dx_ref[...] = ALPHA * dr_ref[...] + dh * (1.0 + mod_ref[sc_row:sc_row + 1, :])
        st_ref[0:1, :] += _rowsum(dh * x_ref[...])
        st_ref[1:2, :] += _rowsum(dh)

    row = pl.BlockSpec((TR, D), lambda i: (i, 0))
    in_specs, args = [], []
    for g, w in parts:
        kk = g.shape[1]
        in_specs += [pl.BlockSpec((TR, kk), lambda i: (i, 0)), _full((D, kk))]
        args += [g, w]
    return pl.pallas_call(
        body, name=name, grid=(s // TR,), in_specs=in_specs + [row, row, _full((8, D))],
        out_specs=[row, _full((8, D))], out_shape=[SDS((s, D), F32), SDS((8, D), F32)], compiler_params=_cp(1),
    )(*args, dr, xin, mod)


def mm_tn(a, b, tm, name):
    s, m = a.shape
    n = b.shape[1]

    def body(a_ref, b_ref, o_ref):
        @pl.when(pl.program_id(1) == 0)
        def _():
            o_ref[...] = jnp.zeros_like(o_ref)

        o_ref[...] += _dot_tn(a_ref[...], b_ref[...])

    return pl.pallas_call(
        body, name=name, grid=(m // tm, s // TR),
        in_specs=[pl.BlockSpec((TR, tm), lambda j, k: (k, j)), pl.BlockSpec((TR, n), lambda j, k: (k, 0))],
        out_specs=pl.BlockSpec((tm, n), lambda j, k: (j, 0)), out_shape=SDS((m, n), F32), compiler_params=_cp(2),
    )(a, b)


MIX_STAT_ROWS = 40


def mix_bwd(z, dymix, hc, scw, pblk, pscale, cw, cvec):
    s = z.shape[0]
    nt = s // TR
    r = TR // HALO
    ext = TR + HALO

    def body(z_ref, zp_ref, zn_ref, dy_ref, dyn_ref, hc_ref, hcn_ref, scw_ref, pblk_ref, ps_ref, cw_ref, cvec_ref,
             dz_ref, st_ref, dp_ref, db_ref, fbuf, bbuf):
        i = pl.program_id(0)

        @pl.when(i == 0)
        def _():
            st_ref[...] = jnp.zeros_like(st_ref)
            dp_ref[...] = jnp.zeros_like(dp_ref)
            db_ref[...] = jnp.zeros_like(db_ref)

        z = z_ref[...]
        zp = jnp.where(i > 0, zp_ref[...], 0.0)
        dy = dy_ref[...]
        dyn = jnp.where(i < nt - 1, dyn_ref[...], 0.0)
        fbuf[0:HALO, 0:256] = zp[:, 512:768] * zp[:, 0:256]
        fbuf[HALO:ext, 0:256] = z[:, 512:768] * z[:, 0:256]
        fbuf[0:HALO, 256:512] = zp[:, 768:1024]
        fbuf[HALO:ext, 256:512] = z[:, 768:1024]
        fbuf[0:HALO, 512:768] = zp[:, 1024:1280] * _sigmoid(zp[:, 1280:1536])
        sg = _sigmoid(z[:, 1280:1536])
        fbuf[HALO:ext, 512:768] = z[:, 1024:1280] * sg

        cv = scw_ref[0:1, :] * fbuf[pl.ds(HALO - 2, TR), 0:256]
        for j in range(1, SC_TAPS):
            cv = cv + scw_ref[j:j + 1, :] * fbuf[pl.ds(HALO - 2 + j, TR), 0:256]
        dz_b = dy[:, 0:256] * cv
        dcv = dy[:, 0:256] * z[:, 256:512]
        bbuf[0:TR, 0:256] = dcv
        bbuf[TR:ext, 0:256] = dyn[:, 0:256] * zn_ref[...]
        da = scw_ref[0:1, :] * bbuf[pl.ds(2, TR), 0:256]
        for j in range(1, SC_TAPS):
            da = da + scw_ref[j:j + 1, :] * bbuf[pl.ds(2 - j, TR), 0:256]
        for j in range(SC_TAPS):
            st_ref[j:j + 1, :] += _rowsum(dcv * fbuf[pl.ds(HALO - 2 + j, TR), 0:256])
        dz_ref[:, 0:256] = (da * z[:, 512:768]).astype(BF16)
        dz_ref[:, 256:512] = dz_b.astype(BF16)
        dz_ref[:, 512:768] = (da * z[:, 0:256]).astype(BF16)
        db_ref[:, 0:256] += _rowsum(da * z[:, 512:768])
        db_ref[:, 256:512] += _rowsum(dz_b)
        db_ref[:, 512:768] += _rowsum(da * z[:, 0:256])

        lane, cnt = _pool_consts(i, ext)
        acc = fbuf[pl.ds(HALO, TR), 256:512]
        caps = {}
        for k in range(1, 16):
            acc = acc + fbuf[pl.ds(HALO - k, TR), 256:512]
            if k + 1 in POOL_WINDOWS:
                caps[k + 1] = acc
        dmean = (_pick_window(lane, caps[2], caps[4], caps[8], caps[16]) / cnt[0:TR, :] - z[:, 768:1024]).astype(BF16)
        o = _dot(dmean, pblk_ref[...])
        st_ref[3:4, :] += _rowsum(dy[:, 256:512] * o)
        do = (jnp.concatenate([dy[:, 256:512], dyn[:, 256:512]], axis=0) * ps_ref[...]).astype(BF16)
        dd = _dot_nt(do, pblk_ref[...])
        bbuf[:, 256:512] = dd / cnt
        dp_ref[...] += _dot_tn(dmean, do[0:TR, :])
        acc = bbuf[pl.ds(0, TR), 256:512]
        caps = {}
        for k in range(1, 16):
            acc = acc + bbuf[pl.ds(k, TR), 256:512]
            if k + 1 in POOL_WINDOWS:
                caps[k + 1] = acc
        dz_p = _pick_window(lane, caps[2], caps[4], caps[8], caps[16]) - dd[0:TR, :]
        dz_ref[:, 768:1024] = dz_p.astype(BF16)
        db_ref[:, 768:1024] += _rowsum(dz_p)

        hce = jnp.concatenate([hc_ref[...], hcn_ref[...]], axis=0)
        dye = jnp.concatenate([dy[:, 512:768], dyn[:, 512:768]], axis=0)
        gam = cvec_ref[1:2, :]
        xhat, rstd = _ln_stats(hce)
        hn = xhat * gam + cvec_ref[2:3, :]
        sh = _sigmoid(hn)
        dhn = dye * (sh * (1.0 + hn * (1.0 - sh)))
        dxh = dhn * gam
        m1 = jnp.mean(dxh, axis=-1, keepdims=True)
        m2 = jnp.mean(dxh * xhat, axis=-1, keepdims=True)
        dhc = rstd * (dxh - m1 - xhat * m2)
        bbuf[:, 512:768] = dhc
        st_ref[4:5, :] += _rowsum((dhn * xhat)[0:TR, :])
        st_ref[5:6, :] += _rowsum(dhn[0:TR, :])
        dhc_t = dhc[0:TR, :]
        st_ref[6:7, :] += _rowsum(dhc_t)
        dhg = cw_ref[0:1, :] * bbuf[pl.ds(30, TR), 512:768]
        st_ref[8:9, :] += _rowsum(dhc_t * fbuf[pl.ds(HALO - 30, TR), 512:768])
        for j in range(1, CF_TAPS):
            dhg = dhg + cw_ref[j:j + 1, :] * bbuf[pl.ds(30 - j, TR), 512:768]
            st_ref[8 + j:9 + j, :] += _rowsum(dhc_t * fbuf[pl.ds(HALO - 30 + j, TR), 512:768])
        dz_v = dhg * sg
        dz_g = dhg * z[:, 1024:1280] * (sg * (1.0 - sg))
        dz_ref[:, 1024:1280] = dz_v.astype(BF16)
        dz_ref[:, 1280:1536] = dz_g.astype(BF16)
        db_ref[:, 1024:1280] += _rowsum(dz_v)
        db_ref[:, 1280:1536] += _rowsum(dz_g)

    nxt = lambda i: jnp.minimum((i + 1) * r, nt * r - 1)
    return pl.pallas_call(
        body, name="mix_bwd", grid=(nt,),
        in_specs=[pl.BlockSpec((TR, 1536), lambda i: (i, 0)),
                  pl.BlockSpec((HALO, 1536), lambda i: (jnp.maximum(i * r - 1, 0), 0)),
                  pl.BlockSpec((HALO, GW), lambda i: (nxt(i), 1)),
                  pl.BlockSpec((TR, 768), lambda i: (i, 0)), pl.BlockSpec((HALO, 768), lambda i: (nxt(i), 0)),
                  pl.BlockSpec((TR, GW), lambda i: (i, 0)), pl.BlockSpec((HALO, GW), lambda i: (nxt(i), 0)),
                  _full((8, GW)), _full((GW, GW)), _full((1, GW)), _full((32, GW)), _full((8, GW))],
        out_specs=[pl.BlockSpec((TR, 1536), lambda i: (i, 0)), _full((MIX_STAT_ROWS, GW)), _full((GW, GW)), _full((1, 1536))],
        out_shape=[SDS((s, IN_W), BF16), SDS((MIX_STAT_ROWS, GW), F32), SDS((GW, GW), F32), SDS((1, 1536), F32)],
        scratch_shapes=[pltpu.VMEM((ext, 768), F32), pltpu.VMEM((ext, 768), F32)], compiler_params=_cp(1),
    )(z, z, z, dymix, dymix, hc, hc, scw, pblk, pscale, cw, cvec)


def ssm_bwd(z, dz, dymix, y, xb, bblk, cblk, pw, dvec, wglu, bglu):
    s = z.shape[0]
    nt = s // TR

    def body(u_ref, dzin, dyd_ref, y_ref, xb_ref, bblk_ref, cblk_ref, pw_ref, dvec_ref, wglu_ref, bglu_ref,
             dzs_ref, dbt_ref, dct_ref, da_ref, dwg_ref, vst_ref, st, gs, carry, gcarry):
        i = pl.program_id(0)

        @pl.when(i == 0)
        def _():
            gcarry[...] = jnp.zeros_like(gcarry)
            for ref in (dbt_ref, dct_ref, da_ref, dwg_ref, vst_ref):
                ref[...] = jnp.zeros_like(ref)

        u = u_ref[...]
        y = y_ref[...]
        yg, th = _gelu(y)
        ygb = yg.astype(BF16)
        q = _dot(ygb, wglu_ref[...]) + bglu_ref[...]
        sq = _sigmoid(q)
        dout = dyd_ref[...]
        dq = dout * yg * (sq * (1.0 - sq))
        dqb = dq.astype(BF16)
        dyg = dout * sq + _dot_nt(dqb, wglu_ref[...])
        dy = dyg * _gelu_grad(y, th)
        dyb = dy.astype(BF16)
        dwg_ref[...] += _dot_tn(ygb, dqb)
        vst_ref[0:1, :] += _rowsum(dq)
        vst_ref[1:2, :] += _rowsum(dy * u)
        ub = u.astype(BF16)
        carry[...] = xb_ref[...]
        st[0:8, :] = xb_ref[...]
        st[8:8 + TR, :] = _dot(ub, bblk_ref[...])
        _scan_rows(st, 8, TR, pw_ref, carry, conj=False, reverse=False)
        gs[...] = _dot_nt(dyb, cblk_ref[...])
        _scan_rows(gs, 0, TR, pw_ref, gcarry, conj=True, reverse=True)
        xs = st[pl.ds(8, TR), :]
        dct_ref[...] += _dot_tn(xs.astype(BF16), dyb)
        g = gs[...]
        gb = g.astype(BF16)
        dbt_ref[...] += _dot_tn(ub, gb)
        gr, gi = g[:, 0:NST], g[:, NST:]
        xp = st[pl.ds(7, TR), :]
        xr, xi = xp[:, 0:NST], xp[:, NST:]
        da_ref[0:1, 0:NST] += _rowsum(gr * xr + gi * xi)
        da_ref[0:1, NST:] += _rowsum(gi * xr - gr * xi)
        du = _dot_nt(gb, bblk_ref[...]) + dvec_ref[...] * dy
        dzs_ref[...] = du.astype(BF16)
        vst_ref[2:3, :] += _rowsum(du)

    rev = lambda i: nt - 1 - i
    return pl.pallas_call(
        body, name="ssm_bwd", grid=(nt,),
        in_specs=[pl.BlockSpec((TR, GW), lambda i: (rev(i), 6)), pl.BlockSpec(memory_space=pl.ANY),
                  pl.BlockSpec((TR, GW), lambda i: (rev(i), 3)), pl.BlockSpec((TR, GW), lambda i: (rev(i), 0)),
                  pl.BlockSpec((8, 2 * NST), lambda i: (rev(i), 0)),
                  _full((GW, 2 * NST)), _full((2 * NST, GW)), _full((8, 2 * NST)), _full((1, GW)),
                  _full((GW, GW)), _full((1, GW))],
        out_specs=[pl.BlockSpec((TR, GW), lambda i: (rev(i), 6)), _full((GW, 2 * NST)), _full((2 * NST, GW)),
                   _full((8, 2 * NST)), _full((GW, GW)), _full((8, GW))],
        out_shape=[SDS((s, IN_W), BF16), SDS((GW, 2 * NST), F32), SDS((2 * NST, GW), F32), SDS((8, 2 * NST), F32),
                   SDS((GW, GW), F32), SDS((8, GW), F32)],
        scratch_shapes=[pltpu.VMEM((8 + TR, 2 * NST), F32), pltpu.VMEM((TR, 2 * NST), F32),
                        pltpu.VMEM((8, 2 * NST), F32), pltpu.VMEM((8, 2 * NST), F32)],
        input_output_aliases={1: 0}, compiler_params=_cp(1),
    )(z, dz, dymix, y, xb, bblk, cblk, pw, dvec, wglu, bglu)


def _ssm_prep(lam_re, lam_im, log_dt, b_re, b_im):
    dt = jnp.exp(log_dt)[:, None]
    mag = jnp.exp(lam_re * dt)
    ar, ai = mag * jnp.cos(lam_im * dt), mag * jnp.sin(lam_im * dt)
    den = lam_re * lam_re + lam_im * lam_im
    qr = ((ar - 1.0) * lam_re + ai * lam_im) / den
    qi = (ai * lam_re - (ar - 1.0) * lam_im) / den
    bbr = qr[..., None] * b_re - qi[..., None] * b_im
    bbi = qr[..., None] * b_im + qi[..., None] * b_re
    return ar, ai, bbr, bbi


def _ssm_tables(lam_re, lam_im, log_dt):
    dt = jnp.exp(log_dt)[:, None]
    k = jnp.arange(1, 9, dtype=F32)[:, None, None]
    mag = jnp.exp(k * (lam_re * dt)[None])
    ang = k * (lam_im * dt)[None]
    return jnp.concatenate([(mag * jnp.cos(ang)).reshape(8, NST), (mag * jnp.sin(ang)).reshape(8, NST)], axis=1)


def _blockdiag(m):
    g, a, b = m.shape
    return jnp.einsum('gab,gk->gakb', m, jnp.eye(g, dtype=m.dtype)).reshape(g * a, g * b)


def _diag_blocks(m, g):
    a, b = m.shape[0] // g, m.shape[1] // g
    return jnp.einsum('gagb->gab', m.reshape(g, a, g, b))


def _pad_rows(m, rows):
    return jnp.pad(m, ((0, rows - m.shape[0]), (0, 0)))


def layer_params(p):
    ar, ai, bbr, bbi = _ssm_prep(p['ssm_lam_re'], p['ssm_lam_im'], p['ssm_log_dt'], p['ssm_b_re'], p['ssm_b_im'])
    bblk = jnp.concatenate([_blockdiag(jnp.swapaxes(bbr, 1, 2)), _blockdiag(jnp.swapaxes(bbi, 1, 2))], axis=1)
    cblk = jnp.concatenate([_blockdiag(jnp.swapaxes(p['ssm_c_re'], 1, 2)),
                            -_blockdiag(jnp.swapaxes(p['ssm_c_im'], 1, 2))], axis=0)
    return dict(
        scw=_pad_rows(p['sc_w'], 8), pblk=_blockdiag(p['pool_w']).astype(BF16), pscale=p['pool_scale'][None],
        cw=_pad_rows(p['cf_dw_w'], 32),
        cvec=_pad_rows(jnp.stack([p['cf_dw_b'], p['cf_ln_g'], p['cf_ln_b']]), 8),
        bblk=bblk.astype(BF16), cblk=cblk.astype(BF16), pw=_ssm_tables(p['ssm_lam_re'], p['ssm_lam_im'], p['ssm_log_dt']),
        dvec=p['ssm_d'][None], bglu=p['ssm_b_glu'][None], b_in=p['b_in'][None],
        ln1_g=p['ln1_g'][None], ln1_b=p['ln1_b'][None], ln2_g=p['ln2_g'][None], ln2_b=p['ln2_b'][None],
    )


def layer_fwd(x, mod, w, q):
    z, h1 = mm_in_fwd(x, mod, w['w_in'], q['b_in'])
    ymix, hc = mix_fwd(z, q['scw'], q['pblk'], q['pscale'], q['cw'], q['cvec'])
    ymix, ys, xb = ssm_fwd(z, ymix, q['bblk'], q['cblk'], q['pw'], q['dvec'], w['ssm_w_glu'], q['bglu'])
    y, r1, x1, h2 = mm_res_ln(ymix, w['w_o'], x, mod, q['ln1_g'], q['ln1_b'], 2, (3, 4))
    gt, up, act = mm_gate_up(h2, w['w_gate'], w['w_up'])
    f, r2, x2 = mm_res_ln(act, w['w_down'], x1, mod, q['ln2_g'], q['ln2_b'], 5, None)
    saved = dict(x=x, z=z, h1=h1, ymix=ymix, hc=hc, ys=ys, xb=xb, y=y, r1=r1, x1=x1, h2=h2, gt=gt, up=up, act=act, f=f, r2=r2)
    return x2, saved


def layer_bwd(dx2, mod, w, q, p, sv):
    dr2, df, dgt, dup, st2 = bwd_ln_down(dx2, sv['r2'], sv['f'], sv['gt'], sv['up'], w['w_down'], mod, q['ln2_g'])
    dx1, stm2 = bwd_dx_mod([(dgt, w['w_gate']), (dup, w['w_up'])], dr2, sv['x1'], mod, 3, 4, "bwd_dx_ff")
    g_down = mm_tn(sv['act'], df, 256, "dw_down")
    g_gate = mm_tn(sv['h2'], dgt, 256, "dw_gate")
    g_up = mm_tn(sv['h2'], dup, 256, "dw_up")
    dr1, dy, dymix, st1 = bwd_ln_o(dx1, sv['r1'], sv['y'], w['w_o'], mod, q['ln1_g'])
    g_o = mm_tn(sv['ymix'], dy, 256, "dw_o")
    dz, mst, dpool, dbin = mix_bwd(sv['z'], dymix, sv['hc'], q['scw'], q['pblk'], q['pscale'], q['cw'], q['cvec'])
    dz, dbt, dct, da, g_glu, vst = ssm_bwd(sv['z'], dz, dymix, sv['ys'], sv['xb'], q['bblk'], q['cblk'], q['pw'],
                                           q['dvec'], w['ssm_w_glu'], q['bglu'])
    dx, stm1 = bwd_dx_mod([(dz, w['w_in'])], dr1, sv['x'], mod, 0, 1, "bwd_dx_in")
    g_in = mm_tn(sv['h1'], dz, 256, "dw_in")
    big = dict(w_in=g_in, w_o=g_o, w_gate=g_gate, w_up=g_up, w_down=g_down, ssm_w_glu=g_glu)
    raw = dict(
        dar=da[0, :NST].reshape(SSM_G, SSM_P), dai=da[0, NST:].reshape(SSM_G, SSM_P),
        dbbr=jnp.swapaxes(_diag_blocks(dbt[:, :NST], SSM_G), 1, 2), dbbi=jnp.swapaxes(_diag_blocks(dbt[:, NST:], SSM_G), 1, 2),
        ssm_c_re=jnp.swapaxes(_diag_blocks(dct[:NST], SSM_G), 1, 2), ssm_c_im=-jnp.swapaxes(_diag_blocks(dct[NST:], SSM_G), 1, 2),
    )
    small = dict(
        b_in=jnp.concatenate([dbin[0], vst[2]]), sc_w=mst[0:3], pool_w=_diag_blocks(dpool, 4), pool_scale=mst[3],
        cf_dw_w=mst[8:8 + CF_TAPS], cf_dw_b=mst[6], cf_ln_g=mst[4], cf_ln_b=mst[5],
        ssm_d=vst[1], ssm_b_glu=vst[0], ln1_g=st1[0], ln1_b=st1[1], ln2_g=st2[0], ln2_b=st2[1], **raw,
    )
    dmod = jnp.stack([stm1[1], stm1[0], st1[2], stm2[1], stm2[0], st2[2]])
    return dx, big, small, dmod


def ssm_param_grads(p, small):
    args = (p['ssm_lam_re'], p['ssm_lam_im'], p['ssm_log_dt'], p['ssm_b_re'], p['ssm_b_im'])
    _, vjp = jax.vjp(_ssm_prep, *args)
    g = vjp((small['dar'], small['dai'], small['dbbr'], small['dbbi']))
    return dict(zip(('ssm_lam_re', 'ssm_lam_im', 'ssm_log_dt', 'ssm_b_re', 'ssm_b_im'), g))


MESH = pl.DeviceIdType.MESH
ANY = pl.BlockSpec(memory_space=pl.ANY)


def _place():
    return lax.axis_index("x"), lax.axis_index("y"), lax.axis_index("c")


def allgather8(v, name):
    m_per, n = v.shape

    def body(x_ref, out_ref, send_sems, recv_sems, local_sem):
        x, y, c = _place()
        me, sibling = (x, y, c), (x, y, 1 - c)
        chips = [(1 - x, y), (x, 1 - y), (1 - x, 1 - y)]

        def rows(px, py, pc):
            return out_ref.at[pl.ds((4 * px + 2 * py + pc) * m_per, m_per), :]

        def copy(k, block, to, src=None):
            return pltpu.make_async_remote_copy(
                src_ref=rows(*block) if src is None else src, dst_ref=rows(*block),
                send_sem=send_sems.at[k], recv_sem=recv_sems.at[k], device_id=to, device_id_type=MESH)

        mine = pltpu.make_async_copy(x_ref, rows(*me), local_sem)
        mine.start()
        first = [copy(0, me, sibling, src=x_ref)]
        first += [copy(1 + j, me, (*chip, c), src=x_ref) for j, chip in enumerate(chips)]
        for cp in first:
            cp.start()
        passed = [copy(4 + j, (*chip, c), sibling) for j, chip in enumerate(chips)]
        for j, chip in enumerate(chips):
            copy(1 + j, (*chip, c), me).wait_recv()
            passed[j].start()
        copy(0, sibling, me).wait_recv()
        for j, chip in enumerate(chips):
            copy(4 + j, (*chip, 1 - c), me).wait_recv()
        for cp in first + passed:
            cp.wait_send()
        mine.wait()

    return pl.pallas_call(
        body, name=name, out_shape=SDS((N_DEV * m_per, n), v.dtype),
        in_specs=[pl.BlockSpec(memory_space=pltpu.VMEM)], out_specs=pl.BlockSpec(memory_space=pltpu.VMEM),
        scratch_shapes=[pltpu.SemaphoreType.DMA((7,)), pltpu.SemaphoreType.DMA((7,)), pltpu.SemaphoreType.DMA],
        compiler_params=pltpu.CompilerParams(vmem_limit_bytes=VMEM_LIMIT),
    )(v)


def pair_swap(g):
    nsh, _, rr, cc = g.shape

    def body(g_ref, out_ref, send_sems, recv_sems):
        x, y, c = _place()
        cps = [pltpu.make_async_remote_copy(
            src_ref=g_ref.at[j, 1 - c], dst_ref=out_ref.at[j], send_sem=send_sems.at[j], recv_sem=recv_sems.at[j],
            device_id=(x, y, 1 - c), device_id_type=MESH) for j in range(nsh)]
        for cp in cps:
            cp.start()
        for cp in cps:
            cp.wait()

    return pl.pallas_call(
        body, name="pair_swap", out_shape=SDS((nsh, rr, cc), g.dtype), in_specs=[ANY], out_specs=ANY,
        scratch_shapes=[pltpu.SemaphoreType.DMA((nsh,)), pltpu.SemaphoreType.DMA((nsh,))],
    )(g)


def chip_scatter(ps):
    nsh, rr, cc = ps.shape

    def body(ps_ref, out_ref, send_sems, recv_sems, local_sem):
        x, y, c = _place()
        mychip = 2 * x + y
        mine = pltpu.make_async_copy(ps_ref.at[mychip], out_ref.at[mychip], local_sem)
        mine.start()
        chips = [(1 - x, y), (x, 1 - y), (1 - x, 1 - y)]
        cps = [pltpu.make_async_remote_copy(
            src_ref=ps_ref.at[2 * kx + ky], dst_ref=out_ref.at[mychip], send_sem=send_sems.at[j],
            recv_sem=recv_sems.at[j], device_id=(kx, ky, c), device_id_type=MESH) for j, (kx, ky) in enumerate(chips)]
        for cp in cps:
            cp.start()
        for cp in cps:
            cp.wait()
        mine.wait()

    return pl.pallas_call(
        body, name="chip_scatter", out_shape=SDS((nsh, rr, cc), ps.dtype), in_specs=[ANY], out_specs=ANY,
        scratch_shapes=[pltpu.SemaphoreType.DMA((3,)), pltpu.SemaphoreType.DMA((3,)), pltpu.SemaphoreType.DMA],
    )(ps)


def pair_gather(t):
    rr, cc = t.shape

    def body(t_ref, out_ref, send_sem, recv_sem, local_sem):
        x, y, c = _place()
        mine = pltpu.make_async_copy(t_ref, out_ref.at[c], local_sem)
        mine.start()
        cp = pltpu.make_async_remote_copy(src_ref=t_ref, dst_ref=out_ref.at[c], send_sem=send_sem, recv_sem=recv_sem,
                                          device_id=(x, y, 1 - c), device_id_type=MESH)
        cp.start()
        cp.wait()
        mine.wait()

    return pl.pallas_call(
        body, name="pair_gather", out_shape=SDS((2, rr, cc), t.dtype), in_specs=[ANY], out_specs=ANY,
        scratch_shapes=[pltpu.SemaphoreType.DMA, pltpu.SemaphoreType.DMA, pltpu.SemaphoreType.DMA],
    )(t)


PACK_C = 512


def _row_tile(rows, cap=512):
    return max(t for t in range(8, cap + 1, 8) if rows % t == 0)


def sum_leading(v, name):
    k, rr, cc = v.shape
    tr = _row_tile(rr)

    def body(v_ref, o_ref):
        acc = v_ref[0]
        for i in range(1, k):
            acc = acc + v_ref[i]
        o_ref[...] = acc

    return pl.pallas_call(
        body, name=name, grid=(rr // tr,), in_specs=[pl.BlockSpec((k, tr, cc), lambda i: (0, i, 0))],
        out_specs=pl.BlockSpec((tr, cc), lambda i: (i, 0)), out_shape=SDS((rr, cc), F32), compiler_params=_cp(1),
    )(v)


def pair_sum(g, recv):
    nsh, _, rr, cc = g.shape
    tr = _row_tile(rr)
    c = lax.axis_index("c")

    def body(c_ref, g_ref, r_ref, o_ref):
        o_ref[...] = g_ref[...] + r_ref[...]

    return pl.pallas_call(
        body, name="pair_sum",
        grid_spec=pltpu.PrefetchScalarGridSpec(
            num_scalar_prefetch=1, grid=(nsh, rr // tr),
            in_specs=[pl.BlockSpec((None, None, tr, cc), lambda j, i, c_ref: (j, c_ref[0], i, 0)),
                      pl.BlockSpec((None, tr, cc), lambda j, i, c_ref: (j, i, 0))],
            out_specs=pl.BlockSpec((None, tr, cc), lambda j, i, c_ref: (j, i, 0))),
        out_shape=SDS((nsh, rr, cc), F32), compiler_params=_cp(2),
    )(jnp.reshape(c, (1,)).astype(jnp.int32), g, recv)


def ada_fwd(c_all, w_ada):
    nl, _, n = w_ada.shape
    bn = 512

    def body(c_ref, w_ref, o_ref):
        cv = c_ref[...]
        cond = (cv * _sigmoid(cv)).astype(BF16)
        o_ref[...] = _dot(cond, w_ref[...].astype(BF16))

    return pl.pallas_call(
        body, name="ada_fwd", grid=(nl, n // bn),
        in_specs=[pl.BlockSpec((8, D), lambda l, j: (0, 0)), pl.BlockSpec((None, D, bn), lambda l, j: (l, 0, j))],
        out_specs=pl.BlockSpec((None, 8, bn), lambda l, j: (l, 0, j)), out_shape=SDS((nl, 8, n), F32),
        compiler_params=_cp(2),
    )(c_all, w_ada)


def ada_grad(c_all, dm):
    nl, _, n = dm.shape
    bn = 512

    def body(c_ref, d_ref, o_ref):
        cv = c_ref[...]
        cond = (cv * _sigmoid(cv)).astype(BF16)
        o_ref[...] = _dot_tn(cond, d_ref[...].astype(BF16))

    return pl.pallas_call(
        body, name="ada_grad", grid=(nl, n // bn),
        in_specs=[pl.BlockSpec((8, D), lambda l, j: (0, 0)), pl.BlockSpec((None, 8, bn), lambda l, j: (l, 0, j))],
        out_specs=pl.BlockSpec((None, D, bn), lambda l, j: (l, 0, j)), out_shape=SDS((nl, D, n), F32),
        compiler_params=_cp(2),
    )(c_all, dm)


def adamw(w, g, m, v, name):
    rr, cc = w.shape
    tr = 256 if rr % 256 == 0 else rr

    def body(w_ref, g_ref, m_ref, v_ref, d_ref, mo_ref, vo_ref):
        g = g_ref[...]
        m = ADAM_B1 * m_ref[...] + (1.0 - ADAM_B1) * g
        v = ADAM_B2 * v_ref[...] + (1.0 - ADAM_B2) * (g * g)
        m_hat = m / (1.0 - ADAM_B1 ** ADAM_STEP)
        v_hat = v / (1.0 - ADAM_B2 ** ADAM_STEP)
        d_ref[...] = -ADAM_LR * (m_hat / (jnp.sqrt(v_hat) + ADAM_EPS) + ADAM_WD * w_ref[...])
        mo_ref[...] = m
        vo_ref[...] = v

    spec = pl.BlockSpec((tr, cc), lambda i: (i, 0))
    return pl.pallas_call(
        body, name=name, grid=(rr // tr,), in_specs=[spec] * 4, out_specs=[spec] * 3,
        out_shape=[SDS((rr, cc), F32)] * 3, compiler_params=_cp(1),
    )(w, g, m, v)


BIG = ('w_in', 'w_o', 'w_gate', 'w_up', 'w_down', 'ssm_w_glu')
BIG_SHARD_AXIS = dict(w_in=1, w_o=0, w_gate=1, w_up=1, w_down=0, ssm_w_glu=0)
BIG_FULL = dict(w_in=(D, IN_W), w_o=(D, D), w_gate=(D, D_FF), w_up=(D, D_FF), w_down=(D_FF, D), ssm_w_glu=(GW, GW))
N_CHIPS = 4
WEIGHTS = ['w_ada', 'b_ada', 'w_in', 'b_in', 'sc_w', 'pool_w', 'pool_scale', 'cf_dw_w', 'cf_dw_b', 'cf_ln_g', 'cf_ln_b',
           'ssm_lam_re', 'ssm_lam_im', 'ssm_log_dt', 'ssm_b_re', 'ssm_b_im', 'ssm_c_re', 'ssm_c_im', 'ssm_d', 'ssm_w_glu',
           'ssm_b_glu', 'w_o', 'ln1_g', 'ln1_b', 'w_gate', 'w_up', 'w_down', 'ln2_g', 'ln2_b']
SMALL = [('b_in', IN_W), ('sc_w', 3 * GW), ('pool_w', 4 * 64 * 64), ('pool_scale', GW), ('cf_dw_w', CF_TAPS * GW),
         ('cf_dw_b', GW), ('cf_ln_g', GW), ('cf_ln_b', GW), ('ssm_d', GW), ('ssm_b_glu', GW),
         ('ln1_g', D), ('ln1_b', D), ('ln2_g', D), ('ln2_b', D), ('dar', NST), ('dai', NST),
         ('dbbr', NST * SSM_H), ('dbbi', NST * SSM_H), ('ssm_c_re', NST * SSM_H), ('ssm_c_im', NST * SSM_H), ('dmod', 6 * D)]


def _shard_shape(name):
    full = list(BIG_FULL[name])
    full[BIG_SHARD_AXIS[name]] //= N_CHIPS
    return tuple(full)


def _pack_rows(flat, cols=PACK_C, mult=8):
    n = flat.shape[0]
    rows = -(-n // cols)
    rows = -(-rows // mult) * mult
    return jnp.pad(flat, (0, rows * cols - n)).reshape(rows, cols)


def _split_shards(name, full):
    ax = BIG_SHARD_AXIS[name]
    parts = jnp.split(full, N_CHIPS, axis=ax)
    return jnp.stack([p.reshape(-1) for p in parts])


def _join_shards(name, flat4):
    shp = _shard_shape(name)
    return jnp.concatenate([flat4[j].reshape(shp) for j in range(N_CHIPS)], axis=BIG_SHARD_AXIS[name])


def kernel(x, c, w_ada, b_ada, w_in, b_in, sc_w, pool_w, pool_scale, cf_dw_w, cf_dw_b, cf_ln_g, cf_ln_b, ssm_lam_re, ssm_lam_im, ssm_log_dt, ssm_b_re, ssm_b_im, ssm_c_re, ssm_c_im, ssm_d, ssm_w_glu, ssm_b_glu, w_o, ln1_g, ln1_b, w_gate, w_up, w_down, ln2_g, ln2_b, loss_target, m_w_ada, m_b_ada, m_w_in, m_b_in, m_sc_w, m_pool_w, m_pool_scale, m_cf_dw_w, m_cf_dw_b, m_cf_ln_g, m_cf_ln_b, m_ssm_lam_re, m_ssm_lam_im, m_ssm_log_dt, m_ssm_b_re, m_ssm_b_im, m_ssm_c_re, m_ssm_c_im, m_ssm_d, m_ssm_w_glu, m_ssm_b_glu, m_w_o, m_ln1_g, m_ln1_b, m_w_gate, m_w_up, m_w_down, m_ln2_g, m_ln2_b, v_w_ada, v_b_ada, v_w_in, v_b_in, v_sc_w, v_pool_w, v_pool_scale, v_cf_dw_w, v_cf_dw_b, v_cf_ln_g, v_cf_ln_b, v_ssm_lam_re, v_ssm_lam_im, v_ssm_log_dt, v_ssm_b_re, v_ssm_b_im, v_ssm_c_re, v_ssm_c_im, v_ssm_d, v_ssm_w_glu, v_ssm_b_glu, v_w_o, v_ln1_g, v_ln1_b, v_w_gate, v_w_up, v_w_down, v_ln2_g, v_ln2_b):
    a = dict(locals())
    W = {n: a[n] for n in WEIGHTS}
    M = {n: a['m_' + n] for n in WEIGHTS}
    V = {n: a['v_' + n] for n in WEIGHTS}
    xi, yi, ci = _place()
    me = 4 * xi + 2 * yi + ci
    chip = 2 * xi + yi
    nl = DEPTH

    pre = jnp.concatenate([c.reshape(-1), sc_w.reshape(-1), cf_dw_w.reshape(-1)])
    pre_all = allgather8(_pack_rows(pre), "gather_pre").reshape(N_DEV, -1)
    c_all = pre_all[:, :D]
    sc_full = jnp.concatenate([pre_all[2 * j, D:D + nl * 3 * 64].reshape(nl, 3, 64) for j in range(N_CHIPS)], axis=-1)
    o2 = D + nl * 3 * 64
    cf_full = jnp.concatenate([pre_all[2 * j, o2:o2 + nl * CF_TAPS * 64].reshape(nl, CF_TAPS, 64) for j in range(N_CHIPS)], axis=-1)

    mod_part = ada_fwd(c_all, w_ada)
    mod_all = allgather8(mod_part.reshape(nl * 8, -1), "gather_mod").reshape(N_DEV, nl, 8, -1)
    mod_all = jnp.concatenate([mod_all[2 * j] for j in range(N_CHIPS)], axis=-1) + b_ada[:, None, :]
    mod_mine = lax.dynamic_index_in_dim(mod_all, me, axis=1, keepdims=False).reshape(nl, 6, D)
    mods = [_pad_rows(mod_mine[l], 8) for l in range(nl)]

    wfull = []
    for l in range(nl):
        flat = jnp.concatenate([W[n][l].reshape(-1) for n in BIG]).astype(BF16).reshape(2, -1, PACK_C)
        half = lax.dynamic_index_in_dim(flat, ci, axis=0, keepdims=False)
        got = allgather8(half, "gather_w").reshape(N_CHIPS, -1)
        wl, off = {}, 0
        for n in BIG:
            sz = int(np.prod(_shard_shape(n)))
            wl[n] = _join_shards(n, got[:, off:off + sz])
            off += sz
        wfull.append(wl)

    P = []
    for l in range(nl):
        p = {n: W[n][l] for n in WEIGHTS if n not in BIG and n not in ('w_ada', 'b_ada', 'sc_w', 'cf_dw_w')}
        p['sc_w'], p['cf_dw_w'] = sc_full[l], cf_full[l]
        P.append(p)
    Q = [layer_params(p) for p in P]
    h = x[0]
    saved = []
    for l in range(nl):
        h, sv = layer_fwd(h, mods[l], wfull[l], Q[l])
        saved.append(sv)
    l8, dh = loss_fwd(h, loss_target[0])
    loss = lax.psum(l8[0, 0], ("x", "y", "c"))

    bigs, smalls = [None] * nl, [None] * nl
    for l in reversed(range(nl)):
        dh, big, small, dmod = layer_bwd(dh, mods[l], wfull[l], Q[l], P[l], saved[l])
        small['dmod'] = dmod
        bigs[l], smalls[l] = big, small
    grad_x = dh[None]

    sflat = jnp.concatenate([smalls[l][n].reshape(-1) for l in range(nl) for n, _ in SMALL])
    spack = _pack_rows(sflat)
    sall = allgather8(spack, "gather_small").reshape(N_DEV, spack.shape[0], PACK_C)
    ssum = sum_leading(sall, "sum_small").reshape(-1)
    dm_all = sall.reshape(N_DEV, -1)
    G = {n: [None] * nl for n in WEIGHTS}
    per_layer = sum(sz for _, sz in SMALL)
    dmod_all = []
    for l in range(nl):
        off = l * per_layer
        red = {}
        for n, sz in SMALL:
            red[n] = ssum[off:off + sz]
            if n == 'dmod':
                dmod_all.append(dm_all[:, off:off + sz])
            off += sz
        red = {n: v.reshape(np.shape(smalls[l][n])) for n, v in red.items()}
        red.update(ssm_param_grads(P[l], red))
        for n in WEIGHTS:
            if n in red:
                G[n][l] = red[n].reshape(W[n].shape[1:]) if n not in ('sc_w', 'cf_dw_w') else \
                    lax.dynamic_slice_in_dim(red[n], chip * 64, 64, axis=-1)
        G['b_ada'][l] = red['dmod'].reshape(-1)
    dmod_all = jnp.stack(dmod_all)
    dm_cols = lax.dynamic_slice_in_dim(dmod_all, chip * (6 * D // N_CHIPS), 6 * D // N_CHIPS, axis=2)
    G['w_ada'] = ada_grad(c_all, dm_cols)

    gpk = jnp.concatenate(
        [jnp.concatenate([_split_shards(n, bigs[l][n]) for n in BIG], axis=1).reshape(N_CHIPS, 2, -1, PACK_C)
         for l in range(nl)], axis=2)
    ps = pair_sum(gpk, pair_swap(gpk))
    tot = sum_leading(chip_scatter(ps), "sum_chips")
    both = pair_gather(tot)
    rl = both.shape[1] // nl
    for l in range(nl):
        flat = both[:, l * rl:(l + 1) * rl].reshape(-1)
        off = 0
        for n in BIG:
            shp = _shard_shape(n)
            sz = int(np.prod(shp))
            G[n][l] = flat[off:off + sz].reshape(shp)
            off += sz
    G = {n: (g if n == 'w_ada' else jnp.stack(g)) for n, g in G.items()}

    big_names = ('w_ada',) + BIG
    delta, new_m, new_v = {}, {}, {}
    for n in big_names:
        shp = W[n].shape
        two = (shp[0] * shp[1], shp[2])
        d_, m_, v_ = adamw(W[n].reshape(two), G[n].reshape(two), M[n].reshape(two), V[n].reshape(two), "adamw_" + n)
        delta[n], new_m[n], new_v[n] = d_.reshape(shp), m_.reshape(shp), v_.reshape(shp)
    rest = [n for n in WEIGHTS if n not in big_names]
    cat = lambda t: _pack_rows(jnp.concatenate([t[n].reshape(-1) for n in rest]))
    d_, m_, v_ = adamw(cat(W), cat(G), cat(M), cat(V), "adamw_small")
    off = 0
    for n in rest:
        sz = int(np.prod(W[n].shape))
        for dst, src in ((delta, d_), (new_m, m_), (new_v, v_)):
            dst[n] = src.reshape(-1)[off:off + sz].reshape(W[n].shape)
        off += sz
    return (loss, grad_x, *[G[n] for n in WEIGHTS], *[delta[n] for n in WEIGHTS],
            *[new_m[n] for n in WEIGHTS], *[new_v[n] for n in WEIGHTS])
```

```python
import math

import jax
import jax.numpy as jnp
import numpy as np
from jax import lax
from jax.experimental import pallas as pl
from jax.experimental.pallas import tpu as pltpu

F32 = jnp.float32
BF16 = jnp.bfloat16
SDS = jax.ShapeDtypeStruct

D = 1024
DEPTH = 4
GW = 256
IN_W = 7 * GW
D_FF = 2816
SC_TAPS = 3
POOL_WINDOWS = (2, 4, 8, 16)
CF_TAPS = 31
SSM_G, SSM_H, SSM_P = 16, 16, 64
NST = SSM_G * SSM_P
ALPHA = (2 * DEPTH) ** 0.25
LN_EPS = 1e-5
ADAM_LR, ADAM_B1, ADAM_B2, ADAM_EPS, ADAM_WD, ADAM_STEP = 0.001, 0.9, 0.999, 1e-08, 0.01, 10

TR = 512
TK = 1024
HALO = 32
FF_BLK = 1408
VMEM_LIMIT = 56 * 1024 * 1024
N_DEV = 8
N_CHIPS = 4

MIXP_ROWS = 56
MIXP_CW, MIXP_CB, MIXP_LG, MIXP_LB, MIXP_PS = 8, 40, 41, 42, 48


def _cp(n_axes):
    return pltpu.CompilerParams(dimension_semantics=("arbitrary",) * n_axes, vmem_limit_bytes=VMEM_LIMIT)


def _sigmoid(x):
    return 1.0 / (1.0 + jnp.exp(-x))


_GELU_C = math.sqrt(2.0 / math.pi)


def _gelu(x):
    t = jnp.tanh(_GELU_C * (x + 0.044715 * (x * x * x)))
    return 0.5 * x * (1.0 + t), t


def _gelu_grad(x, t):
    return 0.5 * (1.0 + t) + 0.5 * x * (1.0 - t * t) * (_GELU_C * (1.0 + 3 * 0.044715 * (x * x)))


def _ln_stats(r):
    mu = jnp.mean(r, axis=-1, keepdims=True)
    xc = r - mu
    var = jnp.mean(xc * xc, axis=-1, keepdims=True)
    rstd = lax.rsqrt(var + LN_EPS)
    return xc * rstd, rstd


def _ln_bwd(dy, r, gamma):
    xhat, rstd = _ln_stats(r)
    dxh = dy * gamma
    m1 = jnp.mean(dxh, axis=-1, keepdims=True)
    m2 = jnp.mean(dxh * xhat, axis=-1, keepdims=True)
    return rstd * (dxh - m1 - xhat * m2), dy * xhat


def _rowsum(v):
    return jnp.sum(v, axis=0, keepdims=True)


def _dot(a, b):
    return jnp.dot(a, b, preferred_element_type=F32)


def _dot_nt(a, b):
    return lax.dot_general(a, b, (((1,), (1,)), ((), ())), preferred_element_type=F32)


def _dot_tn(a, b):
    return lax.dot_general(a, b, (((0,), (0,)), ((), ())), preferred_element_type=F32)


def _full(shape):
    return pl.BlockSpec(shape, lambda *_: (0,) * len(shape))


def mm_in_fwd(x, mod, w_t, b):
    s = x.shape[0]

    def body(x_ref, mod_ref, w_ref, b_ref, z_ref, h_ref):
        h = x_ref[...] * (1.0 + mod_ref[1:2, :]) + mod_ref[0:1, :]
        hb = h.astype(BF16)
        h_ref[...] = hb
        z_ref[...] = _dot_nt(hb, w_ref[...]) + b_ref[...]

    return pl.pallas_call(
        body, name="mm_in_fwd", grid=(s // TR,),
        in_specs=[pl.BlockSpec((TR, D), lambda i: (i, 0)), _full((8, D)), _full((IN_W, D)), _full((1, IN_W))],
        out_specs=[pl.BlockSpec((TR, IN_W), lambda i: (i, 0)), pl.BlockSpec((TR, D), lambda i: (i, 0))],
        out_shape=[SDS((s, IN_W), F32), SDS((s, D), BF16)], compiler_params=_cp(1),
    )(x, mod, w_t, b)


def _pool_consts(i, rows):
    lane = lax.broadcasted_iota(jnp.int32, (1, GW), 1) // (GW // 4)
    wl = jnp.where(lane == 0, 2.0, jnp.where(lane == 1, 4.0, jnp.where(lane == 2, 8.0, 16.0))).astype(F32)
    pos = (i * TR + 1 + lax.broadcasted_iota(jnp.int32, (rows, 1), 0)).astype(F32)
    return lane, jnp.minimum(pos, wl)


def _pick_window(lane, c2, c4, c8, c16):
    return jnp.where(lane == 0, c2, jnp.where(lane == 1, c4, jnp.where(lane == 2, c8, c16)))


def mix_fwd(z, mixp, pblk):
    s = z.shape[0]
    nt = s // TR
    r = TR // HALO

    def body(z_ref, zp_ref, p_ref, pblk_ref, y_ref, hc_ref, buf):
        i = pl.program_id(0)
        z = z_ref[...]
        zp = jnp.where(i > 0, zp_ref[...], 0.0)
        buf[0:HALO, 0:256] = zp[:, 512:768] * zp[:, 0:256]
        buf[HALO:HALO + TR, 0:256] = z[:, 512:768] * z[:, 0:256]
        buf[0:HALO, 256:512] = zp[:, 768:1024]
        buf[HALO:HALO + TR, 256:512] = z[:, 768:1024]
        buf[0:HALO, 512:768] = zp[:, 1024:1280] * _sigmoid(zp[:, 1280:1536])
        buf[HALO:HALO + TR, 512:768] = z[:, 1024:1280] * _sigmoid(z[:, 1280:1536])
        cv = p_ref[0:1, :] * buf[pl.ds(HALO - 2, TR), 0:256]
        for j in range(1, SC_TAPS):
            cv = cv + p_ref[j:j + 1, :] * buf[pl.ds(HALO - 2 + j, TR), 0:256]
        y_ref[:, 0:256] = (z[:, 256:512] * cv).astype(BF16)
        lane, cnt = _pool_consts(i, TR)
        acc = buf[pl.ds(HALO, TR), 256:512]
        caps = {}
        for k in range(1, 16):
            acc = acc + buf[pl.ds(HALO - k, TR), 256:512]
            if k + 1 in POOL_WINDOWS:
                caps[k + 1] = acc
        dmean = _pick_window(lane, caps[2], caps[4], caps[8], caps[16]) / cnt - z[:, 768:1024]
        y_ref[:, 256:512] = (_dot(dmean.astype(BF16), pblk_ref[...]) * p_ref[MIXP_PS:MIXP_PS + 1, :]).astype(BF16)
        hc = p_ref[MIXP_CW:MIXP_CW + 1, :] * buf[pl.ds(HALO - 30, TR), 512:768]
        for j in range(1, CF_TAPS):
            hc = hc + p_ref[MIXP_CW + j:MIXP_CW + j + 1, :] * buf[pl.ds(HALO - 30 + j, TR), 512:768]
        hc = hc + p_ref[MIXP_CB:MIXP_CB + 1, :]
        hc_ref[...] = hc
        xhat, _ = _ln_stats(hc)
        hn = xhat * p_ref[MIXP_LG:MIXP_LG + 1, :] + p_ref[MIXP_LB:MIXP_LB + 1, :]
        y_ref[:, 512:768] = (hn * _sigmoid(hn)).astype(BF16)

    return pl.pallas_call(
        body, name="mix_fwd", grid=(nt,),
        in_specs=[pl.BlockSpec((TR, 1536), lambda i: (i, 0)),
                  pl.BlockSpec((HALO, 1536), lambda i: (jnp.maximum(i * r - 1, 0), 0)),
                  _full((MIXP_ROWS, GW)), _full((GW, GW))],
        out_specs=[pl.BlockSpec((TR, 768), lambda i: (i, 0)), pl.BlockSpec((TR, GW), lambda i: (i, 0))],
        out_shape=[SDS((s, D), BF16), SDS((s, GW), F32)],
        scratch_shapes=[pltpu.VMEM((HALO + TR, 768), F32)], compiler_params=_cp(1),
    )(z, z, mixp, pblk)


SCAN_LANE_TILES = 8


def _scan_rows(st, base, nrows, tab_ref, carry_ref, reverse):
    ng = nrows // 8
    edge = slice(0, 1) if reverse else slice(7, 8)
    for j0 in range(0, NST // 128, SCAN_LANE_TILES):
        cols = [(slice(j * 128, (j + 1) * 128), slice(NST + j * 128, NST + (j + 1) * 128))
                for j in range(j0, j0 + SCAN_LANE_TILES)]

        def group(gi, carry, cols=cols):
            g = (ng - 1 - gi) if reverse else gi
            r0 = pl.multiple_of(base + g * 8, 8)
            out = []
            for (cr, ci), (c_r, c_i) in zip(cols, carry):
                xr = st[pl.ds(r0, 8), cr]
                xi = st[pl.ds(r0, 8), ci]
                for k, sft in enumerate((1, 2, 4)):
                    ar, ai = tab_ref[8 * k:8 * k + 8, cr], tab_ref[8 * k:8 * k + 8, ci]
                    amt = (8 - sft) if reverse else sft
                    rr = pltpu.roll(xr, amt, 0)
                    ri = pltpu.roll(xi, amt, 0)
                    xr, xi = xr + (ar * rr - ai * ri), xi + (ar * ri + ai * rr)
                pr, pi = tab_ref[24:32, cr], tab_ref[24:32, ci]
                xr, xi = xr + (pr * c_r - pi * c_i), xi + (pr * c_i + pi * c_r)
                st[pl.ds(r0, 8), cr] = xr
                st[pl.ds(r0, 8), ci] = xi
                out.append((jnp.broadcast_to(xr[edge, :], (8, 128)), jnp.broadcast_to(xi[edge, :], (8, 128))))
            return tuple(out)

        res = lax.fori_loop(0, ng, group, tuple((carry_ref[:, cr], carry_ref[:, ci]) for cr, ci in cols))
        for (cr, ci), (c_r, c_i) in zip(cols, res):
            carry_ref[:, cr] = c_r
            carry_ref[:, ci] = c_i


def ssm_fwd(z, ymix, bblk, cblk, tabf, ssmv, wglu):
    s = z.shape[0]
    nt = s // TR

    def body(u_ref, ymix_in, bblk_ref, cblk_ref, tab_ref, v_ref, wglu_ref, yd_ref, y_ref, xb_ref, st, carry):
        i = pl.program_id(0)

        @pl.when(i == 0)
        def _():
            carry[...] = jnp.zeros_like(carry)

        xb_ref[...] = carry[...]
        u = u_ref[...]
        st[...] = _dot(u.astype(BF16), bblk_ref[...])
        _scan_rows(st, 0, TR, tab_ref, carry, reverse=False)
        y = _dot(st[...].astype(BF16), cblk_ref[...]) + v_ref[0:1, :] * u
        y_ref[...] = y
        yg, _ = _gelu(y)
        q = _dot(yg.astype(BF16), wglu_ref[...]) + v_ref[1:2, :]
        yd_ref[...] = (yg * _sigmoid(q)).astype(BF16)

    return pl.pallas_call(
        body, name="ssm_fwd", grid=(nt,),
        in_specs=[pl.BlockSpec((TR, GW), lambda i: (i, 6)), pl.BlockSpec(memory_space=pl.ANY),
                  _full((GW, 2 * NST)), _full((2 * NST, GW)), _full((32, 2 * NST)), _full((8, GW)), _full((GW, GW))],
        out_specs=[pl.BlockSpec((TR, GW), lambda i: (i, 3)), pl.BlockSpec((TR, GW), lambda i: (i, 0)),
                   pl.BlockSpec((8, 2 * NST), lambda i: (i, 0))],
        out_shape=[SDS((s, D), BF16), SDS((s, GW), F32), SDS((nt * 8, 2 * NST), F32)],
        scratch_shapes=[pltpu.VMEM((TR, 2 * NST), F32), pltpu.VMEM((8, 2 * NST), F32)],
        input_output_aliases={1: 0}, compiler_params=_cp(1),
    )(z, ymix, bblk, cblk, tabf, ssmv, wglu)


def mm_res_ln(a, w, xres, mod, lnp, g_row, ln_row, h2_rows):
    s, k = a.shape

    def body(a_ref, w_ref, x_ref, mod_ref, ln_ref, f_ref, r_ref, xo_ref, *h_ref):
        f = _dot(a_ref[...], w_ref[...])
        f_ref[...] = f
        r = ALPHA * x_ref[...] + (1.0 + mod_ref[g_row:g_row + 1, :]) * f
        r_ref[...] = r
        xhat, _ = _ln_stats(r)
        xo = xhat * ln_ref[ln_row:ln_row + 1, :] + ln_ref[ln_row + 1:ln_row + 2, :]
        xo_ref[...] = xo
        if h2_rows is not None:
            sh, sc = h2_rows
            h_ref[0][...] = (xo * (1.0 + mod_ref[sc:sc + 1, :]) + mod_ref[sh:sh + 1, :]).astype(BF16)

    row = pl.BlockSpec((TR, D), lambda i: (i, 0))
    outs = [SDS((s, D), F32)] * 3 + ([SDS((s, D), BF16)] if h2_rows is not None else [])
    return pl.pallas_call(
        body, name="mm_res_ln_%d" % k, grid=(s // TR,),
        in_specs=[pl.BlockSpec((TR, k), lambda i: (i, 0)), _full((k, D)), row, _full((8, D)), _full((8, D))],
        out_specs=[row] * len(outs), out_shape=outs, compiler_params=_cp(1),
    )(a, w, xres, mod, lnp)


def mm_gate_up(h2, wg_t, wu_t):
    s = h2.shape[0]
    nn = D_FF // FF_BLK

    def body(h_ref, wg_ref, wu_ref, gt_ref, up_ref, a_ref):
        h = h_ref[...]
        gt = _dot_nt(h, wg_ref[...])
        up = _dot_nt(h, wu_ref[...])
        gt_ref[...] = gt
        up_ref[...] = up
        a_ref[...] = (gt * _sigmoid(gt) * up).astype(BF16)

    wspec = pl.BlockSpec((FF_BLK, D), lambda n, i: (n, 0))
    ospec = pl.BlockSpec((TR, FF_BLK), lambda n, i: (i, n))
    return pl.pallas_call(
        body, name="mm_gate_up", grid=(nn, s // TR),
        in_specs=[pl.BlockSpec((TR, D), lambda n, i: (i, 0)), wspec, wspec],
        out_specs=[ospec, ospec, ospec],
        out_shape=[SDS((s, D_FF), F32), SDS((s, D_FF), F32), SDS((s, D_FF), BF16)], compiler_params=_cp(2),
    )(h2, wg_t, wu_t)


def loss_fwd(x, target):
    s = x.shape[0]

    def body(x_ref, t_ref, l_ref, dx_ref):
        i = pl.program_id(0)

        @pl.when(i == 0)
        def _():
            l_ref[...] = jnp.zeros_like(l_ref)

        e = x_ref[...] - t_ref[...]
        dx_ref[...] = e * (1.0 / D)
        part = jnp.sum(jnp.mean(e * e, axis=-1, keepdims=True), axis=0, keepdims=True)
        l_ref[...] = l_ref[...] + 0.5 * part

    row = pl.BlockSpec((TR, D), lambda i: (i, 0))
    return pl.pallas_call(
        body, name="loss_fwd", grid=(s // TR,), in_specs=[row, row], out_specs=[_full((8, 128)), row],
        out_shape=[SDS((8, 128), F32), SDS((s, D), F32)], compiler_params=_cp(1),
    )(x, target)


def bwd_ln_o(dx, r, y, w_o, mod, lnp):
    s = dx.shape[0]

    def body(dx_ref, r_ref, y_ref, w_ref, mod_ref, ln_ref, dr_ref, dy_ref, dm_ref, st_ref):
        i = pl.program_id(0)

        @pl.when(i == 0)
        def _():
            st_ref[...] = jnp.zeros_like(st_ref)

        dx = dx_ref[...]
        dr, dg = _ln_bwd(dx, r_ref[...], ln_ref[0:1, :])
        dr_ref[...] = dr
        dy = ((1.0 + mod_ref[2:3, :]) * dr).astype(BF16)
        dy_ref[...] = dy
        dm_ref[...] = _dot_nt(dy, w_ref[...])
        st_ref[0:1, :] += _rowsum(dr * y_ref[...])
        st_ref[1:2, :] += _rowsum(dg)
        st_ref[2:3, :] += _rowsum(dx)

    row = pl.BlockSpec((TR, D), lambda i: (i, 0))
    return pl.pallas_call(
        body, name="bwd_ln_o", grid=(s // TR,),
        in_specs=[row, row, row, _full((D, D)), _full((8, D)), _full((8, D))],
        out_specs=[row, row, row, _full((8, D))],
        out_shape=[SDS((s, D), F32), SDS((s, D), BF16), SDS((s, D), F32), SDS((8, D), F32)], compiler_params=_cp(1),
    )(dx, r, y, w_o, mod, lnp)


def bwd_ln_down(dx, r, f, gt, up, w_down, mod, lnp):
    s = dx.shape[0]
    nn = D_FF // FF_BLK

    def body(dx_ref, r_ref, f_ref, gt_ref, up_ref, w_ref, mod_ref, ln_ref, dr_ref, df_ref, dgt_ref, dup_ref, st_ref):
        i = pl.program_id(0)
        n = pl.program_id(1)

        @pl.when((i == 0) & (n == 0))
        def _():
            st_ref[...] = jnp.zeros_like(st_ref)

        @pl.when(n == 0)
        def _():
            dx = dx_ref[...]
            dr, dg = _ln_bwd(dx, r_ref[...], ln_ref[2:3, :])
            dr_ref[...] = dr
            df_ref[...] = ((1.0 + mod_ref[5:6, :]) * dr).astype(BF16)
            st_ref[0:1, :] += _rowsum(dr * f_ref[...])
            st_ref[1:2, :] += _rowsum(dg)
            st_ref[2:3, :] += _rowsum(dx)

        da = _dot_nt(df_ref[...], w_ref[...])
        gt = gt_ref[...]
        sg = _sigmoid(gt)
        dgt_ref[...] = (da * up_ref[...] * (sg * (1.0 + gt * (1.0 - sg)))).astype(BF16)
        dup_ref[...] = (da * (gt * sg)).astype(BF16)

    row = pl.BlockSpec((TR, D), lambda i, n: (i, 0))
    ff = pl.BlockSpec((TR, FF_BLK), lambda i, n: (i, n))
    par = pl.BlockSpec((8, D), lambda i, n: (0, 0))
    return pl.pallas_call(
        body, name="bwd_ln_down", grid=(s // TR, nn),
        in_specs=[row, row, row, ff, ff, pl.BlockSpec((FF_BLK, D), lambda i, n: (n, 0)), par, par],
        out_specs=[row, row, ff, ff, par],
        out_shape=[SDS((s, D), F32), SDS((s, D), BF16), SDS((s, D_FF), BF16), SDS((s, D_FF), BF16), SDS((8, D), F32)],
        compiler_params=_cp(2),
    )(dx, r, f, gt, up, w_down, mod, lnp)


def bwd_dx_mod(parts, dr, xin, mod, sh_row, sc_row, name):
    s = dr.shape[0]
    npart = len(parts)

    def body(*refs):
        ins, (dx_ref, st_ref) = refs[:2 * npart + 3], refs[2 * npart + 3:]
        dr_ref, x_ref, mod_ref = ins[2 * npart:]
        i = pl.program_id(0)

        @pl.when(i == 0)
        def _():
            st_ref[...] = jnp.zeros_like(st_ref)

        dh = _dot(ins[0][...], ins[1][...])
        for k in range(1, npart):
            dh = dh + _dot(ins[2 * k][...], ins[2 * k + 1][...])
        dx_ref[...] = ALPHA * dr_ref[...] + dh * (1.0 + mod_ref[sc_row:sc_row + 1, :])
        st_ref[0:1, :] += _rowsum(dh)
        st_ref[1:2, :] += _rowsum(dh * x_ref[...])

    row = pl.BlockSpec((TR, D), lambda i: (i, 0))
    in_specs, args = [], []
    for g, w in parts:
        kk = g.shape[1]
        in_specs += [pl.BlockSpec((TR, kk), lambda i: (i, 0)), _full((kk, D))]
        args += [g, w]
    return pl.pallas_call(
        body, name=name, grid=(s // TR,), in_specs=in_specs + [row, row, _full((8, D))],
        out_specs=[row, _full((8, D))], out_shape=[SDS((s, D), F32), SDS((8, D), F32)], compiler_params=_cp(1),
    )(*args, dr, xin, mod)


def mm_tn(a, b, tm, name):
    s, m = a.shape
    n = b.shape[1]
    tk = min(TK, s)

    def body(a_ref, b_ref, o_ref):
        @pl.when(pl.program_id(1) == 0)
        def _():
            o_ref[...] = jnp.zeros_like(o_ref)

        o_ref[...] += _dot_tn(a_ref[...], b_ref[...])

    return pl.pallas_call(
        body, name=name, grid=(m // tm, s // tk),
        in_specs=[pl.BlockSpec((tk, tm), lambda j, k: (k, j)), pl.BlockSpec((tk, n), lambda j, k: (k, 0))],
        out_specs=pl.BlockSpec((tm, n), lambda j, k: (j, 0)), out_shape=SDS((m, n), F32), compiler_params=_cp(2),
    )(a, b)


def mix_bwd(z, dymix, hc, mixp, pblk):
    s = z.shape[0]
    nt = s // TR
    r = TR // HALO
    ext = TR + HALO

    def body(z_ref, zp_ref, zn_ref, dy_ref, dyn_ref, hc_ref, hcn_ref, p_ref, pblk_ref,
             dz_ref, st_ref, dp_ref, db_ref, fbuf, bbuf, dpacc):
        i = pl.program_id(0)

        @pl.when(i == 0)
        def _():
            st_ref[...] = jnp.zeros_like(st_ref)
            dpacc[...] = jnp.zeros_like(dpacc)
            db_ref[...] = jnp.zeros_like(db_ref)

        z = z_ref[...]
        zp = jnp.where(i > 0, zp_ref[...], 0.0)
        dy = dy_ref[...]
        dyn = jnp.where(i < nt - 1, dyn_ref[...], 0.0)
        fbuf[0:HALO, 0:256] = zp[:, 512:768] * zp[:, 0:256]
        fbuf[HALO:ext, 0:256] = z[:, 512:768] * z[:, 0:256]
        fbuf[0:HALO, 256:512] = zp[:, 768:1024]
        fbuf[HALO:ext, 256:512] = z[:, 768:1024]
        fbuf[0:HALO, 512:768] = zp[:, 1024:1280] * _sigmoid(zp[:, 1280:1536])
        sg = _sigmoid(z[:, 1280:1536])
        fbuf[HALO:ext, 512:768] = z[:, 1024:1280] * sg

        cv = p_ref[0:1, :] * fbuf[pl.ds(HALO - 2, TR), 0:256]
        for j in range(1, SC_TAPS):
            cv = cv + p_ref[j:j + 1, :] * fbuf[pl.ds(HALO - 2 + j, TR), 0:256]
        dz_b = dy[:, 0:256] * cv
        dcv = dy[:, 0:256] * z[:, 256:512]
        bbuf[0:TR, 0:256] = dcv
        bbuf[TR:ext, 0:256] = dyn[:, 0:256] * zn_ref[...]
        da = p_ref[0:1, :] * bbuf[pl.ds(2, TR), 0:256]
        for j in range(1, SC_TAPS):
            da = da + p_ref[j:j + 1, :] * bbuf[pl.ds(2 - j, TR), 0:256]
        for j in range(SC_TAPS):
            st_ref[j:j + 1, :] += _rowsum(dcv * fbuf[pl.ds(HALO - 2 + j, TR), 0:256])
        dz_ref[:, 0:256] = (da * z[:, 512:768]).astype(BF16)
        dz_ref[:, 256:512] = dz_b.astype(BF16)
        dz_ref[:, 512:768] = (da * z[:, 0:256]).astype(BF16)
        db_ref[0:1, 0:256] += _rowsum(da * z[:, 512:768])
        db_ref[0:1, 256:512] += _rowsum(dz_b)
        db_ref[0:1, 512:768] += _rowsum(da * z[:, 0:256])

        ps = p_ref[MIXP_PS:MIXP_PS + 1, :]
        lane, cnt = _pool_consts(i, ext)
        acc = fbuf[pl.ds(HALO, TR), 256:512]
        caps = {}
        for k in range(1, 16):
            acc = acc + fbuf[pl.ds(HALO - k, TR), 256:512]
            if k + 1 in POOL_WINDOWS:
                caps[k + 1] = acc
        dmean = (_pick_window(lane, caps[2], caps[4], caps[8], caps[16]) / cnt[0:TR, :] - z[:, 768:1024]).astype(BF16)
        o = _dot(dmean, pblk_ref[...])
        st_ref[MIXP_PS:MIXP_PS + 1, :] += _rowsum(dy[:, 256:512] * o)
        do = (jnp.concatenate([dy[:, 256:512], dyn[:, 256:512]], axis=0) * ps).astype(BF16)
        dd = _dot_nt(do, pblk_ref[...])
        bbuf[:, 256:512] = dd / cnt
        dpacc[...] += _dot_tn(dmean, do[0:TR, :])
        acc = bbuf[pl.ds(0, TR), 256:512]
        caps = {}
        for k in range(1, 16):
            acc = acc + bbuf[pl.ds(k, TR), 256:512]
            if k + 1 in POOL_WINDOWS:
                caps[k + 1] = acc
        dz_p = _pick_window(lane, caps[2], caps[4], caps[8], caps[16]) - dd[0:TR, :]
        dz_ref[:, 768:1024] = dz_p.astype(BF16)
        db_ref[0:1, 768:1024] += _rowsum(dz_p)

        hce = jnp.concatenate([hc_ref[...], hcn_ref[...]], axis=0)
        dye = jnp.concatenate([dy[:, 512:768], dyn[:, 512:768]], axis=0)
        gam = p_ref[MIXP_LG:MIXP_LG + 1, :]
        xhat, rstd = _ln_stats(hce)
        hn = xhat * gam + p_ref[MIXP_LB:MIXP_LB + 1, :]
        sh = _sigmoid(hn)
        dhn = dye * (sh * (1.0 + hn * (1.0 - sh)))
        dxh = dhn * gam
        m1 = jnp.mean(dxh, axis=-1, keepdims=True)
        m2 = jnp.mean(dxh * xhat, axis=-1, keepdims=True)
        dhc = rstd * (dxh - m1 - xhat * m2)
        bbuf[:, 512:768] = dhc
        st_ref[MIXP_LG:MIXP_LG + 1, :] += _rowsum((dhn * xhat)[0:TR, :])
        st_ref[MIXP_LB:MIXP_LB + 1, :] += _rowsum(dhn[0:TR, :])
        dhc_t = dhc[0:TR, :]
        st_ref[MIXP_CB:MIXP_CB + 1, :] += _rowsum(dhc_t)
        dhg = p_ref[MIXP_CW:MIXP_CW + 1, :] * bbuf[pl.ds(30, TR), 512:768]
        st_ref[MIXP_CW:MIXP_CW + 1, :] += _rowsum(dhc_t * fbuf[pl.ds(HALO - 30, TR), 512:768])
        for j in range(1, CF_TAPS):
            dhg = dhg + p_ref[MIXP_CW + j:MIXP_CW + j + 1, :] * bbuf[pl.ds(30 - j, TR), 512:768]
            st_ref[MIXP_CW + j:MIXP_CW + j + 1, :] += _rowsum(dhc_t * fbuf[pl.ds(HALO - 30 + j, TR), 512:768])
        dz_v = dhg * sg
        dz_g = dhg * z[:, 1024:1280] * (sg * (1.0 - sg))
        dz_ref[:, 1024:1280] = dz_v.astype(BF16)
        dz_ref[:, 1280:1536] = dz_g.astype(BF16)
        db_ref[0:1, 1024:1280] += _rowsum(dz_v)
        db_ref[0:1, 1280:1536] += _rowsum(dz_g)

        @pl.when(i == nt - 1)
        def _():
            for k in range(4):
                dp_ref[64 * k:64 * k + 64, :] = dpacc[64 * k:64 * k + 64, 64 * k:64 * k + 64]

    nxt = lambda i: jnp.minimum((i + 1) * r, nt * r - 1)
    return pl.pallas_call(
        body, name="mix_bwd", grid=(nt,),
        in_specs=[pl.BlockSpec((TR, 1536), lambda i: (i, 0)),
                  pl.BlockSpec((HALO, 1536), lambda i: (jnp.maximum(i * r - 1, 0), 0)),
                  pl.BlockSpec((HALO, GW), lambda i: (nxt(i), 1)),
                  pl.BlockSpec((TR, 768), lambda i: (i, 0)), pl.BlockSpec((HALO, 768), lambda i: (nxt(i), 0)),
                  pl.BlockSpec((TR, GW), lambda i: (i, 0)), pl.BlockSpec((HALO, GW), lambda i: (nxt(i), 0)),
                  _full((MIXP_ROWS, GW)), _full((GW, GW))],
        out_specs=[pl.BlockSpec((TR, 1536), lambda i: (i, 0)), _full((MIXP_ROWS, GW)), _full((GW, 64)), _full((8, 1536))],
        out_shape=[SDS((s, IN_W), BF16), SDS((MIXP_ROWS, GW), F32), SDS((GW, 64), F32), SDS((8, 1536), F32)],
        scratch_shapes=[pltpu.VMEM((ext, 768), F32), pltpu.VMEM((ext, 768), F32), pltpu.VMEM((GW, GW), F32)],
        compiler_params=_cp(1),
    )(z, z, z, dymix, dymix, hc, hc, mixp, pblk)


def ssm_bwd(z, dz, dymix, y, xb, bblk, cblk, tabf, tabb, ssmv, wglu):
    s = z.shape[0]
    nt = s // TR

    def body(u_ref, dzin, dyd_ref, y_ref, xb_ref, bblk_ref, cblk_ref, tabf_ref, tabb_ref, v_ref, wglu_ref,
             dzs_ref, dbc_ref, dcc_ref, da_ref, dwg_ref, vst_ref, st, gs, carry, gcarry, dbacc, dcacc):
        i = pl.program_id(0)

        @pl.when(i == 0)
        def _():
            gcarry[...] = jnp.zeros_like(gcarry)
            for ref in (dbacc, dcacc, da_ref, dwg_ref, vst_ref):
                ref[...] = jnp.zeros_like(ref)

        u = u_ref[...]
        y = y_ref[...]
        yg, th = _gelu(y)
        ygb = yg.astype(BF16)
        q = _dot(ygb, wglu_ref[...]) + v_ref[1:2, :]
        sq = _sigmoid(q)
        dout = dyd_ref[...]
        dq = dout * yg * (sq * (1.0 - sq))
        dqb = dq.astype(BF16)
        dyg = dout * sq + _dot_nt(dqb, wglu_ref[...])
        dy = dyg * _gelu_grad(y, th)
        dyb = dy.astype(BF16)
        dwg_ref[...] += _dot_tn(ygb, dqb)
        vst_ref[0:1, :] += _rowsum(dy * u)
        vst_ref[1:2, :] += _rowsum(dq)
        ub = u.astype(BF16)
        carry[...] = xb_ref[...]
        st[0:8, :] = xb_ref[...]
        st[8:8 + TR, :] = _dot(ub, bblk_ref[...])
        _scan_rows(st, 8, TR, tabf_ref, carry, reverse=False)
        gs[...] = _dot_nt(dyb, cblk_ref[...])
        _scan_rows(gs, 0, TR, tabb_ref, gcarry, reverse=True)
        xs = st[pl.ds(8, TR), :]
        dcacc[...] += _dot_tn(dyb, xs.astype(BF16))
        g = gs[...]
        gb = g.astype(BF16)
        dbacc[...] += _dot_tn(ub, gb)
        gr, gi = g[:, 0:NST], g[:, NST:]
        xp = st[pl.ds(7, TR), :]
        xr, xi = xp[:, 0:NST], xp[:, NST:]
        da_ref[0:1, 0:NST] += _rowsum(gr * xr + gi * xi)
        da_ref[0:1, NST:] += _rowsum(gi * xr - gr * xi)
        du = _dot_nt(gb, bblk_ref[...]) + v_ref[0:1, :] * dy
        dzs_ref[...] = du.astype(BF16)
        vst_ref[2:3, :] += _rowsum(du)

        @pl.when(i == nt - 1)
        def _():
            for g_ in range(SSM_G):
                rows = slice(g_ * SSM_H, (g_ + 1) * SSM_H)
                for acc, out in ((dbacc, dbc_ref), (dcacc, dcc_ref)):
                    out[rows, 0:SSM_P] = acc[rows, g_ * SSM_P:(g_ + 1) * SSM_P]
                    out[rows, SSM_P:2 * SSM_P] = acc[rows, NST + g_ * SSM_P:NST + (g_ + 1) * SSM_P]

    rev = lambda i: nt - 1 - i
    return pl.pallas_call(
        body, name="ssm_bwd", grid=(nt,),
        in_specs=[pl.BlockSpec((TR, GW), lambda i: (rev(i), 6)), pl.BlockSpec(memory_space=pl.ANY),
                  pl.BlockSpec((TR, GW), lambda i: (rev(i), 3)), pl.BlockSpec((TR, GW), lambda i: (rev(i), 0)),
                  pl.BlockSpec((8, 2 * NST), lambda i: (rev(i), 0)),
                  _full((GW, 2 * NST)), _full((2 * NST, GW)), _full((32, 2 * NST)), _full((32, 2 * NST)),
                  _full((8, GW)), _full((GW, GW))],
        out_specs=[pl.BlockSpec((TR, GW), lambda i: (rev(i), 6)), _full((GW, 2 * SSM_P)), _full((GW, 2 * SSM_P)),
                   _full((8, 2 * NST)), _full((GW, GW)), _full((8, GW))],
        out_shape=[SDS((s, IN_W), BF16), SDS((GW, 2 * SSM_P), F32), SDS((GW, 2 * SSM_P), F32), SDS((8, 2 * NST), F32),
                   SDS((GW, GW), F32), SDS((8, GW), F32)],
        scratch_shapes=[pltpu.VMEM((8 + TR, 2 * NST), F32), pltpu.VMEM((TR, 2 * NST), F32),
                        pltpu.VMEM((8, 2 * NST), F32), pltpu.VMEM((8, 2 * NST), F32),
                        pltpu.VMEM((GW, 2 * NST), F32), pltpu.VMEM((GW, 2 * NST), F32)],
        input_output_aliases={1: 0}, compiler_params=_cp(1),
    )(z, dz, dymix, y, xb, bblk, cblk, tabf, tabb, ssmv, wglu)


def _ssm_prep(lam_re, lam_im, log_dt, b_re, b_im):
    dt = jnp.exp(log_dt)[..., None]
    mag = jnp.exp(lam_re * dt)
    ar, ai = mag * jnp.cos(lam_im * dt), mag * jnp.sin(lam_im * dt)
    den = lam_re * lam_re + lam_im * lam_im
    qr = ((ar - 1.0) * lam_re + ai * lam_im) / den
    qi = (ai * lam_re - (ar - 1.0) * lam_im) / den
    bbr = qr[..., None] * b_re - qi[..., None] * b_im
    bbi = qr[..., None] * b_im + qi[..., None] * b_re
    return ar, ai, bbr, bbi


def _ssm_tables(lam_re, lam_im, log_dt):
    nl = lam_re.shape[0]
    dt = jnp.exp(log_dt)[:, None, :, None]
    k = jnp.arange(1, 9, dtype=F32)[None, :, None, None]
    mag = jnp.exp(k * (lam_re[:, None] * dt))
    ang = k * (lam_im[:, None] * dt)
    pr = (mag * jnp.cos(ang)).reshape(nl, 8, NST)
    pi = (mag * jnp.sin(ang)).reshape(nl, 8, NST)
    row = jnp.arange(8)[None, :, None]

    def table(sign, reverse):
        parts_r, parts_i = [], []
        for sft in (1, 2, 4):
            keep = (row < 8 - sft) if reverse else (row >= sft)
            parts_r.append(jnp.where(keep, pr[:, sft - 1:sft], 0.0))
            parts_i.append(jnp.where(keep, sign * pi[:, sft - 1:sft], 0.0))
        parts_r.append(pr[:, ::-1] if reverse else pr)
        parts_i.append(sign * (pi[:, ::-1] if reverse else pi))
        return jnp.concatenate([jnp.concatenate(parts_r, axis=1), jnp.concatenate(parts_i, axis=1)], axis=2)

    return table(1.0, False), table(-1.0, True)


def _blockdiag(m):
    nl, g, a, b = m.shape
    return jnp.einsum('lgab,gk->lgakb', m, jnp.eye(g, dtype=m.dtype)).reshape(nl, g * a, g * b)


def _rows_at(blocks, total):
    out, at = [], 0
    nl, _, c = blocks[0][1].shape
    for r0, b in blocks:
        if r0 > at:
            out.append(jnp.zeros((nl, r0 - at, c), F32))
        out.append(b)
        at = r0 + b.shape[1]
    if total > at:
        out.append(jnp.zeros((nl, total - at, c), F32))
    return jnp.concatenate(out, axis=1)


def prep_params(p):
    ar, ai, bbr, bbi = _ssm_prep(p['ssm_lam_re'], p['ssm_lam_im'], p['ssm_log_dt'], p['ssm_b_re'], p['ssm_b_im'])
    bblk = jnp.concatenate([_blockdiag(jnp.swapaxes(bbr, 2, 3)), _blockdiag(jnp.swapaxes(bbi, 2, 3))], axis=2)
    cblk = jnp.concatenate([_blockdiag(jnp.swapaxes(p['ssm_c_re'], 2, 3)),
                            -_blockdiag(jnp.swapaxes(p['ssm_c_im'], 2, 3))], axis=1)
    tabf, tabb = _ssm_tables(p['ssm_lam_re'], p['ssm_lam_im'], p['ssm_log_dt'])
    mixp = _rows_at([(0, p['sc_w']), (MIXP_CW, p['cf_dw_w']), (MIXP_CB, p['cf_dw_b'][:, None]),
                     (MIXP_LG, p['cf_ln_g'][:, None]), (MIXP_LB, p['cf_ln_b'][:, None]),
                     (MIXP_PS, p['pool_scale'][:, None])], MIXP_ROWS)
    return dict(
        mixp=mixp, pblk=_blockdiag(p['pool_w']).astype(BF16), bblk=bblk.astype(BF16), cblk=cblk.astype(BF16),
        tabf=tabf, tabb=tabb, ssmv=_rows_at([(0, p['ssm_d'][:, None]), (1, p['ssm_b_glu'][:, None])], 8),
        lnp=_rows_at([(0, p['ln1_g'][:, None]), (1, p['ln1_b'][:, None]), (2, p['ln2_g'][:, None]),
                      (3, p['ln2_b'][:, None])], 8),
        b_in=p['b_in'][:, None],
    )


def layer_fwd(x, mod, w, q):
    z, h1 = mm_in_fwd(x, mod, w['w_in'], q['b_in'])
    ymix, hc = mix_fwd(z, q['mixp'], q['pblk'])
    ymix, ys, xb = ssm_fwd(z, ymix, q['bblk'], q['cblk'], q['tabf'], q['ssmv'], w['ssm_w_glu'])
    y, r1, x1, h2 = mm_res_ln(ymix, w['w_o'], x, mod, q['lnp'], 2, 0, (3, 4))
    gt, up, act = mm_gate_up(h2, w['w_gate'], w['w_up'])
    f, r2, x2 = mm_res_ln(act, w['w_down'], x1, mod, q['lnp'], 5, 2, None)
    saved = dict(x=x, z=z, h1=h1, ymix=ymix, hc=hc, ys=ys, xb=xb, y=y, r1=r1, x1=x1, h2=h2, gt=gt, up=up, act=act, f=f, r2=r2)
    return x2, saved


def layer_bwd(dx2, mod, w, q, sv):
    dr2, df, dgt, dup, st2 = bwd_ln_down(dx2, sv['r2'], sv['f'], sv['gt'], sv['up'], w['w_down'], mod, q['lnp'])
    dx1, stm2 = bwd_dx_mod([(dgt, w['w_gate']), (dup, w['w_up'])], dr2, sv['x1'], mod, 3, 4, "bwd_dx_ff")
    g_down = mm_tn(sv['act'], df, FF_BLK, "dw_down")
    g_gate = mm_tn(dgt, sv['h2'], FF_BLK, "dw_gate")
    g_up = mm_tn(dup, sv['h2'], FF_BLK, "dw_up")
    dr1, dy, dymix, st1 = bwd_ln_o(dx1, sv['r1'], sv['y'], w['w_o'], mod, q['lnp'])
    g_o = mm_tn(sv['ymix'], dy, D // 2, "dw_o")
    dz, mst, dpool, dbin = mix_bwd(sv['z'], dymix, sv['hc'], q['mixp'], q['pblk'])
    dz, dbc, dcc, da, g_glu, vst = ssm_bwd(sv['z'], dz, dymix, sv['ys'], sv['xb'], q['bblk'], q['cblk'], q['tabf'],
                                           q['tabb'], q['ssmv'], w['ssm_w_glu'])
    dx, stm1 = bwd_dx_mod([(dz, w['w_in'])], dr1, sv['x'], mod, 0, 1, "bwd_dx_in")
    g_in = mm_tn(dz, sv['h1'], IN_W // 2, "dw_in")
    big = [g_in, g_o, g_gate, g_up, g_down, g_glu]
    stats = dict(s1024=jnp.concatenate([stm1, st1, stm2, st2], axis=0), mst=mst, vst=vst, dpool=dpool,
                 dbc=dbc, dcc=dcc, da=da, dbin=dbin)
    return dx, big, stats


MESH = pl.DeviceIdType.MESH
ANY = pl.BlockSpec(memory_space=pl.ANY)
VMEM_SPEC = pl.BlockSpec(memory_space=pltpu.VMEM)


def _place():
    return lax.axis_index("x"), lax.axis_index("y"), lax.axis_index("c")


def allgather8(vs, name):
    na = len(vs)

    def body(*refs):
        x_refs, out_refs = refs[:na], refs[na:2 * na]
        send_sems, recv_sems, local_sems = refs[2 * na:]
        x, y, c = _place()
        me, sibling = (x, y, c), (x, y, 1 - c)
        chips = [(1 - x, y), (x, 1 - y), (1 - x, 1 - y)]

        def rows(a, px, py, pc):
            m_per = vs[a].shape[0]
            return out_refs[a].at[pl.ds((4 * px + 2 * py + pc) * m_per, m_per), :]

        def copy(a, k, block, to, src=None):
            return pltpu.make_async_remote_copy(
                src_ref=rows(a, *block) if src is None else src, dst_ref=rows(a, *block),
                send_sem=send_sems.at[7 * a + k], recv_sem=recv_sems.at[7 * a + k], device_id=to, device_id_type=MESH)

        mine = [pltpu.make_async_copy(x_refs[a], rows(a, *me), local_sems.at[a]) for a in range(na)]
        for cp in mine:
            cp.start()
        first = []
        for a in range(na):
            first.append(copy(a, 0, me, sibling, src=x_refs[a]))
            first += [copy(a, 1 + j, me, (*chip, c), src=x_refs[a]) for j, chip in enumerate(chips)]
        for cp in first:
            cp.start()
        passed = []
        for j, chip in enumerate(chips):
            for a in range(na):
                copy(a, 1 + j, (*chip, c), me).wait_recv()
                fw = copy(a, 4 + j, (*chip, c), sibling)
                fw.start()
                passed.append(fw)
        for a in range(na):
            copy(a, 0, sibling, me).wait_recv()
            for j, chip in enumerate(chips):
                copy(a, 4 + j, (*chip, 1 - c), me).wait_recv()
        for cp in first + passed:
            cp.wait_send()
        for cp in mine:
            cp.wait()

    return pl.pallas_call(
        body, name=name, out_shape=[SDS((N_DEV * v.shape[0], v.shape[1]), v.dtype) for v in vs],
        in_specs=[VMEM_SPEC] * na, out_specs=[VMEM_SPEC] * na,
        scratch_shapes=[pltpu.SemaphoreType.DMA((7 * na,)), pltpu.SemaphoreType.DMA((7 * na,)),
                        pltpu.SemaphoreType.DMA((na,))],
        compiler_params=pltpu.CompilerParams(vmem_limit_bytes=VMEM_LIMIT),
    )(*vs)


def pair_swap(gs):
    na = len(gs)

    def body(*refs):
        g_refs, out_refs, (send_sems, recv_sems) = refs[:na], refs[na:2 * na], refs[2 * na:]
        x, y, c = _place()
        cps = [pltpu.make_async_remote_copy(
            src_ref=g_refs[a].at[j, 1 - c], dst_ref=out_refs[a].at[j], send_sem=send_sems.at[4 * a + j],
            recv_sem=recv_sems.at[4 * a + j], device_id=(x, y, 1 - c), device_id_type=MESH)
            for a in range(na) for j in range(N_CHIPS)]
        for cp in cps:
            cp.start()
        for cp in cps:
            cp.wait()

    return pl.pallas_call(
        body, name="pair_swap", out_shape=[SDS((g.shape[0],) + g.shape[2:], g.dtype) for g in gs],
        in_specs=[ANY] * na, out_specs=[ANY] * na,
        scratch_shapes=[pltpu.SemaphoreType.DMA((4 * na,)), pltpu.SemaphoreType.DMA((4 * na,))],
    )(*gs)


def chip_scatter(pss):
    na = len(pss)

    def body(*refs):
        ps_refs, out_refs, (send_sems, recv_sems) = refs[:na], refs[na:2 * na], refs[2 * na:]
        x, y, c = _place()
        chips = [(1 - x, y), (x, 1 - y), (1 - x, 1 - y)]
        cps = [pltpu.make_async_remote_copy(
            src_ref=ps_refs[a].at[2 * kx + ky], dst_ref=out_refs[a].at[j], send_sem=send_sems.at[3 * a + j],
            recv_sem=recv_sems.at[3 * a + j], device_id=(kx, ky, c), device_id_type=MESH)
            for a in range(na) for j, (kx, ky) in enumerate(chips)]
        for cp in cps:
            cp.start()
        for cp in cps:
            cp.wait()

    return pl.pallas_call(
        body, name="chip_scatter", out_shape=[SDS((3,) + p.shape[1:], p.dtype) for p in pss],
        in_specs=[ANY] * na, out_specs=[ANY] * na,
        scratch_shapes=[pltpu.SemaphoreType.DMA((3 * na,)), pltpu.SemaphoreType.DMA((3 * na,))],
    )(*pss)


def pair_gather(ts):
    na = len(ts)

    def body(*refs):
        t_refs, out_refs, (send_sems, recv_sems) = refs[:na], refs[na:2 * na], refs[2 * na:]
        x, y, c = _place()
        cps = [pltpu.make_async_remote_copy(
            src_ref=t_refs[a].at[c], dst_ref=out_refs[a].at[c], send_sem=send_sems.at[a], recv_sem=recv_sems.at[a],
            device_id=(x, y, 1 - c), device_id_type=MESH) for a in range(na)]
        for cp in cps:
            cp.start()
        for cp in cps:
            cp.wait()

    return pl.pallas_call(
        body, name="pair_gather", out_shape=[SDS(t.shape, t.dtype) for t in ts],
        in_specs=[ANY] * na, out_specs=[ANY] * na, input_output_aliases={a: a for a in range(na)},
        scratch_shapes=[pltpu.SemaphoreType.DMA((na,)), pltpu.SemaphoreType.DMA((na,))],
    )(*ts)


def _scalar(v):
    return jnp.reshape(v, (1,)).astype(jnp.int32)


def pair_sum(gs, recvs):
    na = len(gs)
    c = lax.axis_index("c")

    def body(c_ref, *refs):
        for a in range(na):
            refs[2 * na + a][...] = (refs[a][...] + refs[na + a][...]).astype(BF16)

    in_specs = [pl.BlockSpec((None, None) + g.shape[2:], lambda j, c_ref: (j, c_ref[0], 0, 0)) for g in gs]
    in_specs += [pl.BlockSpec((None,) + r.shape[1:], lambda j, c_ref: (j, 0, 0)) for r in recvs]
    return pl.pallas_call(
        body, name="pair_sum",
        grid_spec=pltpu.PrefetchScalarGridSpec(
            num_scalar_prefetch=1, grid=(N_CHIPS,), in_specs=in_specs,
            out_specs=[pl.BlockSpec((None,) + r.shape[1:], lambda j, c_ref: (j, 0, 0)) for r in recvs]),
        out_shape=[SDS(r.shape, BF16) for r in recvs], compiler_params=_cp(1),
    )(_scalar(c), *gs, *recvs)


def sum_chips(pss, recvs):
    na = len(pss)
    xi, yi, ci = _place()

    def body(s_ref, *refs):
        for a in range(na):
            r = refs[na + a]
            acc = refs[a][...].astype(F32) + r[0].astype(F32)
            acc = acc + r[1].astype(F32)
            refs[2 * na + a][...] = acc + r[2].astype(F32)

    in_specs = [pl.BlockSpec((None,) + p.shape[1:], lambda i, s_ref: (s_ref[0], 0, 0)) for p in pss]
    in_specs += [pl.BlockSpec(r.shape, lambda i, s_ref: (0, 0, 0)) for r in recvs]
    return pl.pallas_call(
        body, name="sum_chips",
        grid_spec=pltpu.PrefetchScalarGridSpec(
            num_scalar_prefetch=1, grid=(1,), in_specs=in_specs,
            out_specs=[pl.BlockSpec((None,) + p.shape[1:], lambda i, s_ref: (s_ref[1], 0, 0)) for p in pss]),
        out_shape=[SDS((2,) + p.shape[1:], F32) for p in pss], compiler_params=_cp(1),
    )(jnp.stack([2 * xi + yi, ci]).astype(jnp.int32), *pss, *recvs)


def sum8(vs, rows):
    na = len(vs)

    def body(*refs):
        for a in range(na):
            m = rows[a]
            acc = refs[a][0:m, :]
            for d in range(1, N_DEV):
                acc = acc + refs[a][d * m:(d + 1) * m, :]
            refs[na + a][...] = acc

    return pl.pallas_call(
        body, name="sum8", out_shape=[SDS((rows[a], vs[a].shape[1]), F32) for a in range(na)],
        in_specs=[VMEM_SPEC] * na, out_specs=[VMEM_SPEC] * na,
        compiler_params=pltpu.CompilerParams(vmem_limit_bytes=VMEM_LIMIT),
    )(*vs)


def ada_fwd(c_all, w_ada):
    nl, _, n = w_ada.shape
    bn = 512

    def body(c_ref, w_ref, o_ref):
        cv = c_ref[...]
        cond = (cv * _sigmoid(cv)).astype(BF16)
        o_ref[...] = _dot(cond, w_ref[...].astype(BF16))

    return pl.pallas_call(
        body, name="ada_fwd", grid=(nl, n // bn),
        in_specs=[pl.BlockSpec((8, D), lambda l, j: (0, 0)), pl.BlockSpec((None, D, bn), lambda l, j: (l, 0, j))],
        out_specs=pl.BlockSpec((None, 8, bn), lambda l, j: (l, 0, j)), out_shape=SDS((nl, 8, n), F32),
        compiler_params=_cp(2),
    )(c_all, w_ada)


def ada_grad(c_all, dm):
    nl, _, n = dm.shape
    bn = 512

    def body(c_ref, d_ref, o_ref):
        cv = c_ref[...]
        cond = (cv * _sigmoid(cv)).astype(BF16)
        o_ref[...] = _dot_tn(cond, d_ref[...].astype(BF16))

    return pl.pallas_call(
        body, name="ada_grad", grid=(nl, n // bn),
        in_specs=[pl.BlockSpec((8, D), lambda l, j: (0, 0)), pl.BlockSpec((None, 8, bn), lambda l, j: (l, 0, j))],
        out_specs=pl.BlockSpec((None, D, bn), lambda l, j: (l, 0, j)), out_shape=SDS((nl, D, n), F32),
        compiler_params=_cp(2),
    )(c_all, dm)


def _adam_math(w, g, m, v):
    m = ADAM_B1 * m + (1.0 - ADAM_B1) * g
    v = ADAM_B2 * v + (1.0 - ADAM_B2) * (g * g)
    m_hat = m / (1.0 - ADAM_B1 ** ADAM_STEP)
    v_hat = v / (1.0 - ADAM_B2 ** ADAM_STEP)
    return -ADAM_LR * (m_hat / (jnp.sqrt(v_hat) + ADAM_EPS) + ADAM_WD * w), m, v


def adamw(w, g, m, v, name):
    rr, cc = w.shape
    tr = 256

    def body(w_ref, g_ref, m_ref, v_ref, d_ref, mo_ref, vo_ref):
        d_ref[...], mo_ref[...], vo_ref[...] = _adam_math(w_ref[...], g_ref[...], m_ref[...], v_ref[...])

    spec = pl.BlockSpec((tr, cc), lambda i: (i, 0))
    return pl.pallas_call(
        body, name=name, grid=(rr // tr,), in_specs=[spec] * 4, out_specs=[spec] * 3,
        out_shape=[SDS((rr, cc), F32)] * 3, compiler_params=_cp(1),
    )(w, g, m, v)


def adamw_many(ws, gs, ms, vs):
    na = len(ws)

    def body(*refs):
        for a in range(na):
            d, m, v = _adam_math(refs[a][...], refs[na + a][...], refs[2 * na + a][...], refs[3 * na + a][...])
            refs[4 * na + a][...] = d
            refs[5 * na + a][...] = m
            refs[6 * na + a][...] = v

    outs = [SDS(w.shape, F32) for w in ws] * 3
    res = pl.pallas_call(
        body, name="adamw_small", out_shape=outs, in_specs=[VMEM_SPEC] * (4 * na), out_specs=[VMEM_SPEC] * (3 * na),
        compiler_params=pltpu.CompilerParams(vmem_limit_bytes=VMEM_LIMIT),
    )(*ws, *gs, *ms, *vs)
    return res[:na], res[na:2 * na], res[2 * na:]


BIG = ('w_in', 'w_o', 'w_gate', 'w_up', 'w_down', 'ssm_w_glu')
BIG_T = ('w_in', 'w_gate', 'w_up')
WEIGHTS = ['w_ada', 'b_ada', 'w_in', 'b_in', 'sc_w', 'pool_w', 'pool_scale', 'cf_dw_w', 'cf_dw_b', 'cf_ln_g', 'cf_ln_b',
           'ssm_lam_re', 'ssm_lam_im', 'ssm_log_dt', 'ssm_b_re', 'ssm_b_im', 'ssm_c_re', 'ssm_c_im', 'ssm_d', 'ssm_w_glu',
           'ssm_b_glu', 'w_o', 'ln1_g', 'ln1_b', 'w_gate', 'w_up', 'w_down', 'ln2_g', 'ln2_b']
STAT_KINDS = ('s1024', 'mst', 'vst', 'dpool', 'dbc', 'dcc', 'da', 'dbin')


def _chip_blocks(g, width):
    m = g.shape[0] // N_DEV
    return g.reshape(N_CHIPS, 2, m, g.shape[1])[:, 0, :, :width]


def kernel(x, c, w_ada, b_ada, w_in, b_in, sc_w, pool_w, pool_scale, cf_dw_w, cf_dw_b, cf_ln_g, cf_ln_b, ssm_lam_re, ssm_lam_im, ssm_log_dt, ssm_b_re, ssm_b_im, ssm_c_re, ssm_c_im, ssm_d, ssm_w_glu, ssm_b_glu, w_o, ln1_g, ln1_b, w_gate, w_up, w_down, ln2_g, ln2_b, loss_target, m_w_ada, m_b_ada, m_w_in, m_b_in, m_sc_w, m_pool_w, m_pool_scale, m_cf_dw_w, m_cf_dw_b, m_cf_ln_g, m_cf_ln_b, m_ssm_lam_re, m_ssm_lam_im, m_ssm_log_dt, m_ssm_b_re, m_ssm_b_im, m_ssm_c_re, m_ssm_c_im, m_ssm_d, m_ssm_w_glu, m_ssm_b_glu, m_w_o, m_ln1_g, m_ln1_b, m_w_gate, m_w_up, m_w_down, m_ln2_g, m_ln2_b, v_w_ada, v_b_ada, v_w_in, v_b_in, v_sc_w, v_pool_w, v_pool_scale, v_cf_dw_w, v_cf_dw_b, v_cf_ln_g, v_cf_ln_b, v_ssm_lam_re, v_ssm_lam_im, v_ssm_log_dt, v_ssm_b_re, v_ssm_b_im, v_ssm_c_re, v_ssm_c_im, v_ssm_d, v_ssm_w_glu, v_ssm_b_glu, v_w_o, v_ln1_g, v_ln1_b, v_w_gate, v_w_up, v_w_down, v_ln2_g, v_ln2_b):
    a = dict(locals())
    W = {n: a[n] for n in WEIGHTS}
    M = {n: a['m_' + n] for n in WEIGHTS}
    V = {n: a['v_' + n] for n in WEIGHTS}
    xi, yi, ci = _place()
    me = 4 * xi + 2 * yi + ci
    chip = 2 * xi + yi
    nl = DEPTH

    g_c, g_sc, g_cf = allgather8(
        [jnp.pad(c, ((0, 7), (0, 0))), jnp.pad(sc_w.reshape(nl * SC_TAPS, 64), ((0, 4), (0, 0))),
         jnp.pad(cf_dw_w.reshape(nl * CF_TAPS, 64), ((0, 4), (0, 0)))], "gather_pre")
    c_all = g_c.reshape(N_DEV, 8, D)[:, 0]
    sc_full = jnp.moveaxis(_chip_blocks(g_sc, 64)[:, :nl * SC_TAPS].reshape(N_CHIPS, nl, SC_TAPS, 64), 0, 2)
    sc_full = sc_full.reshape(nl, SC_TAPS, GW)
    cf_full = jnp.moveaxis(_chip_blocks(g_cf, 64)[:, :nl * CF_TAPS].reshape(N_CHIPS, nl, CF_TAPS, 64), 0, 2)
    cf_full = cf_full.reshape(nl, CF_TAPS, GW)

    mod_part = ada_fwd(c_all, w_ada)
    (g_mod,) = allgather8([mod_part.reshape(nl * 8, -1)], "gather_mod")
    mod_all = jnp.moveaxis(_chip_blocks(g_mod, 6 * D // N_CHIPS).reshape(N_CHIPS, nl, 8, -1), 0, 2)
    mod_all = mod_all.reshape(nl, 8, 6 * D) + b_ada[:, None, :]
    mod_mine = lax.dynamic_index_in_dim(mod_all, me, axis=1, keepdims=False).reshape(nl, 6, D)
    mods = jnp.pad(mod_mine, ((0, 0), (0, 2), (0, 0)))

    halves = {}
    for n in BIG:
        wt = jnp.swapaxes(W[n], 1, 2) if n in BIG_T else W[n]
        hr = wt.shape[1] // 2
        halves[n] = lax.dynamic_slice_in_dim(wt, ci * hr, hr, axis=1).astype(BF16)
    wfull = []
    for l in range(nl):
        got = allgather8([halves[n][l] for n in BIG], "gather_w")
        wfull.append(dict(zip(BIG, got)))

    small_names = [n for n in WEIGHTS if n not in BIG and n not in ('w_ada', 'b_ada')]
    pfull = {n: W[n] for n in small_names}
    pfull['sc_w'], pfull['cf_dw_w'] = sc_full, cf_full
    Q = prep_params(pfull)
    h = x[0]
    saved = []
    for l in range(nl):
        h, sv = layer_fwd(h, mods[l], wfull[l], {k: v[l] for k, v in Q.items()})
        saved.append(sv)
    l8, dh = loss_fwd(h, loss_target[0])
    loss = lax.psum(l8[0, 0], ("x", "y", "c"))

    stats = [None] * nl
    gbig = {n: [None] * nl for n in BIG}
    for l in reversed(range(nl)):
        dh, big, stats[l] = layer_bwd(dh, mods[l], wfull[l], {k: v[l] for k, v in Q.items()}, saved[l])
        views = [g.reshape(N_CHIPS, 2, g.shape[0] // (2 * N_CHIPS), g.shape[1]) for g in big]
        pss = pair_sum(views, pair_swap(views))
        tot = pair_gather(sum_chips(pss, chip_scatter(pss)))
        for n, t in zip(BIG, tot):
            gbig[n][l] = t.reshape(2 * t.shape[1], t.shape[2])
    grad_x = dh[None]

    mine = [jnp.concatenate([stats[l][k] for l in range(nl)], axis=0) for k in STAT_KINDS]
    rows = [v.shape[0] for v in mine]
    gathered = allgather8(mine, "gather_small")
    S = dict(zip(STAT_KINDS, [v.reshape(nl, r // nl, v.shape[1]) for v, r in zip(sum8(gathered, rows), rows)]))
    G = {}
    s1 = S['s1024']
    mod_rows = (0, 1, 8, 16, 17, 24)
    G['b_ada'] = jnp.concatenate([s1[:, r] for r in mod_rows], axis=1)
    G['ln1_g'], G['ln1_b'], G['ln2_g'], G['ln2_b'] = s1[:, 9], s1[:, 10], s1[:, 25], s1[:, 26]
    mst = S['mst']
    sc_g, cf_g = mst[:, 0:SC_TAPS], mst[:, MIXP_CW:MIXP_CW + CF_TAPS]
    G['sc_w'] = lax.dynamic_slice_in_dim(sc_g, chip * 64, 64, axis=2)
    G['cf_dw_w'] = lax.dynamic_slice_in_dim(cf_g, chip * 64, 64, axis=2)
    G['cf_dw_b'], G['cf_ln_g'], G['cf_ln_b'], G['pool_scale'] = mst[:, MIXP_CB], mst[:, MIXP_LG], mst[:, MIXP_LB], mst[:, MIXP_PS]
    G['pool_w'] = S['dpool'].reshape(nl, 4, 64, 64)
    G['ssm_d'], G['ssm_b_glu'] = S['vst'][:, 0], S['vst'][:, 1]
    G['b_in'] = jnp.concatenate([S['dbin'][:, 0], S['vst'][:, 2]], axis=1)
    dbc = S['dbc'].reshape(nl, SSM_G, SSM_H, 2, SSM_P)
    dcc = S['dcc'].reshape(nl, SSM_G, SSM_H, 2, SSM_P)
    G['ssm_c_re'], G['ssm_c_im'] = dcc[:, :, :, 0], -dcc[:, :, :, 1]
    da = S['da'][:, 0]
    cot = (da[:, :NST].reshape(nl, SSM_G, SSM_P), da[:, NST:].reshape(nl, SSM_G, SSM_P),
           jnp.swapaxes(dbc[:, :, :, 0], 2, 3), jnp.swapaxes(dbc[:, :, :, 1], 2, 3))
    ssm_in = ('ssm_lam_re', 'ssm_lam_im', 'ssm_log_dt', 'ssm_b_re', 'ssm_b_im')
    _, vjp = jax.vjp(_ssm_prep, *[W[n] for n in ssm_in])
    G.update(dict(zip(ssm_in, vjp(cot))))
    g1 = gathered[0].reshape(N_DEV, nl, 32, D)
    dmod_all = jnp.concatenate([g1[:, :, r] for r in mod_rows], axis=2)
    dm_cols = lax.dynamic_slice_in_dim(jnp.swapaxes(dmod_all, 0, 1), chip * (6 * D // N_CHIPS), 6 * D // N_CHIPS, axis=2)
    G['w_ada'] = ada_grad(c_all, dm_cols)
    for n in BIG:
        g = jnp.stack(gbig[n])
        G[n] = jnp.swapaxes(g, 1, 2) if n in BIG_T else g

    big_names = ('w_ada',) + BIG
    delta, new_m, new_v = {}, {}, {}
    for n in big_names:
        shp = W[n].shape
        two = (shp[0] * shp[1], shp[2])
        d_, m_, v_ = adamw(W[n].reshape(two), G[n].reshape(two), M[n].reshape(two), V[n].reshape(two), "adamw_" + n)
        delta[n], new_m[n], new_v[n] = d_.reshape(shp), m_.reshape(shp), v_.reshape(shp)
    rest = [n for n in WEIGHTS if n not in big_names]
    for n in rest:
        G[n] = G[n].reshape(W[n].shape)
    lane_view = lambda t: t.reshape(t.shape[:-2] + (-1,)) if t.shape[-1] < 64 and t.ndim == 4 else t
    ds, ms, vs_ = adamw_many(*[[lane_view(t[n]) for n in rest] for t in (W, G, M, V)])
    for n, d_, m_, v_ in zip(rest, ds, ms, vs_):
        delta[n], new_m[n], new_v[n] = d_.reshape(W[n].shape), m_.reshape(W[n].shape), v_.reshape(W[n].shape)
    return (loss, grad_x, *[G[n] for n in WEIGHTS], *[delta[n] for n in WEIGHTS],
            *[new_m[n] for n in WEIGHTS], *[new_v[n] for n in WEIGHTS])
```

```python
import math

import jax
import jax.numpy as jnp
import numpy as np
from jax import lax
from jax.experimental import pallas as pl
from jax.experimental.pallas import tpu as pltpu

F32 = jnp.float32
BF16 = jnp.bfloat16
SDS = jax.ShapeDtypeStruct

D = 1024
DEPTH = 4
GW = 256
IN_W = 7 * GW
D_FF = 2816
SC_TAPS = 3
POOL_WINDOWS = (2, 4, 8, 16)
CF_TAPS = 31
SSM_G, SSM_H, SSM_P = 16, 16, 64
NST = SSM_G * SSM_P
ALPHA = (2 * DEPTH) ** 0.25
LN_EPS = 1e-5
ADAM_LR, ADAM_B1, ADAM_B2, ADAM_EPS, ADAM_WD, ADAM_STEP = 0.001, 0.9, 0.999, 1e-08, 0.01, 10

TR = 512
TK = 1024
HALO = 32
FF_BLK = 1408
VMEM_LIMIT = 56 * 1024 * 1024
N_DEV = 8
N_CHIPS = 4

MIXP_ROWS = 56
MIXP_CW, MIXP_CB, MIXP_LG, MIXP_LB, MIXP_PS = 8, 40, 41, 42, 48


def _cp(n_axes):
    return pltpu.CompilerParams(dimension_semantics=("arbitrary",) * n_axes, vmem_limit_bytes=VMEM_LIMIT)


def _sigmoid(x):
    return 1.0 / (1.0 + jnp.exp(-x))


_GELU_C = math.sqrt(2.0 / math.pi)


def _gelu(x):
    t = jnp.tanh(_GELU_C * (x + 0.044715 * (x * x * x)))
    return 0.5 * x * (1.0 + t), t


def _gelu_grad(x, t):
    return 0.5 * (1.0 + t) + 0.5 * x * (1.0 - t * t) * (_GELU_C * (1.0 + 3 * 0.044715 * (x * x)))


def _ln_stats(r):
    mu = jnp.mean(r, axis=-1, keepdims=True)
    xc = r - mu
    var = jnp.mean(xc * xc, axis=-1, keepdims=True)
    rstd = lax.rsqrt(var + LN_EPS)
    return xc * rstd, rstd


def _ln_bwd(dy, r, gamma):
    xhat, rstd = _ln_stats(r)
    dxh = dy * gamma
    m1 = jnp.mean(dxh, axis=-1, keepdims=True)
    m2 = jnp.mean(dxh * xhat, axis=-1, keepdims=True)
    return rstd * (dxh - m1 - xhat * m2), dy * xhat


def _rowsum(v):
    return jnp.sum(v, axis=0, keepdims=True)


def _dot(a, b):
    return jnp.dot(a, b, preferred_element_type=F32)


def _dot_nt(a, b):
    return lax.dot_general(a, b, (((1,), (1,)), ((), ())), preferred_element_type=F32)


def _dot_tn(a, b):
    return lax.dot_general(a, b, (((0,), (0,)), ((), ())), preferred_element_type=F32)


def _full(shape):
    return pl.BlockSpec(shape, lambda *_: (0,) * len(shape))


def mm_in_fwd(x, mod, w_t, b):
    s = x.shape[0]

    def body(x_ref, mod_ref, w_ref, b_ref, z_ref, h_ref):
        h = x_ref[...] * (1.0 + mod_ref[1:2, :]) + mod_ref[0:1, :]
        hb = h.astype(BF16)
        h_ref[...] = hb
        z_ref[...] = _dot_nt(hb, w_ref[...]) + b_ref[...]

    return pl.pallas_call(
        body, name="mm_in_fwd", grid=(s // TR,),
        in_specs=[pl.BlockSpec((TR, D), lambda i: (i, 0)), _full((8, D)), _full((IN_W, D)), _full((1, IN_W))],
        out_specs=[pl.BlockSpec((TR, IN_W), lambda i: (i, 0)), pl.BlockSpec((TR, D), lambda i: (i, 0))],
        out_shape=[SDS((s, IN_W), F32), SDS((s, D), BF16)], compiler_params=_cp(1),
    )(x, mod, w_t, b)


def _pool_consts(i, rows):
    lane = lax.broadcasted_iota(jnp.int32, (1, GW), 1) // (GW // 4)
    wl = jnp.where(lane == 0, 2.0, jnp.where(lane == 1, 4.0, jnp.where(lane == 2, 8.0, 16.0))).astype(F32)
    pos = (i * TR + 1 + lax.broadcasted_iota(jnp.int32, (rows, 1), 0)).astype(F32)
    return lane, jnp.minimum(pos, wl)


def _pick_window(lane, c2, c4, c8, c16):
    return jnp.where(lane == 0, c2, jnp.where(lane == 1, c4, jnp.where(lane == 2, c8, c16)))


def mix_fwd(z, mixp, pblk):
    s = z.shape[0]
    nt = s // TR
    r = TR // HALO

    def body(z_ref, zp_ref, p_ref, pblk_ref, y_ref, hc_ref, buf):
        i = pl.program_id(0)
        z = z_ref[...]
        zp = jnp.where(i > 0, zp_ref[...], 0.0)
        buf[0:HALO, 0:256] = zp[:, 512:768] * zp[:, 0:256]
        buf[HALO:HALO + TR, 0:256] = z[:, 512:768] * z[:, 0:256]
        buf[0:HALO, 256:512] = zp[:, 768:1024]
        buf[HALO:HALO + TR, 256:512] = z[:, 768:1024]
        buf[0:HALO, 512:768] = zp[:, 1024:1280] * _sigmoid(zp[:, 1280:1536])
        buf[HALO:HALO + TR, 512:768] = z[:, 1024:1280] * _sigmoid(z[:, 1280:1536])
        cv = p_ref[0:1, :] * buf[pl.ds(HALO - 2, TR), 0:256]
        for j in range(1, SC_TAPS):
            cv = cv + p_ref[j:j + 1, :] * buf[pl.ds(HALO - 2 + j, TR), 0:256]
        y_ref[:, 0:256] = (z[:, 256:512] * cv).astype(BF16)
        lane, cnt = _pool_consts(i, TR)
        acc = buf[pl.ds(HALO, TR), 256:512]
        caps = {}
        for k in range(1, 16):
            acc = acc + buf[pl.ds(HALO - k, TR), 256:512]
            if k + 1 in POOL_WINDOWS:
                caps[k + 1] = acc
        dmean = _pick_window(lane, caps[2], caps[4], caps[8], caps[16]) / cnt - z[:, 768:1024]
        y_ref[:, 256:512] = (_dot(dmean.astype(BF16), pblk_ref[...]) * p_ref[MIXP_PS:MIXP_PS + 1, :]).astype(BF16)
        hc = p_ref[MIXP_CW:MIXP_CW + 1, :] * buf[pl.ds(HALO - 30, TR), 512:768]
        for j in range(1, CF_TAPS):
            hc = hc + p_ref[MIXP_CW + j:MIXP_CW + j + 1, :] * buf[pl.ds(HALO - 30 + j, TR), 512:768]
        hc = hc + p_ref[MIXP_CB:MIXP_CB + 1, :]
        hc_ref[...] = hc
        xhat, _ = _ln_stats(hc)
        hn = xhat * p_ref[MIXP_LG:MIXP_LG + 1, :] + p_ref[MIXP_LB:MIXP_LB + 1, :]
        y_ref[:, 512:768] = (hn * _sigmoid(hn)).astype(BF16)

    return pl.pallas_call(
        body, name="mix_fwd", grid=(nt,),
        in_specs=[pl.BlockSpec((TR, 1536), lambda i: (i, 0)),
                  pl.BlockSpec((HALO, 1536), lambda i: (jnp.maximum(i * r - 1, 0), 0)),
                  _full((MIXP_ROWS, GW)), _full((GW, GW))],
        out_specs=[pl.BlockSpec((TR, 768), lambda i: (i, 0)), pl.BlockSpec((TR, GW), lambda i: (i, 0))],
        out_shape=[SDS((s, D), BF16), SDS((s, GW), F32)],
        scratch_shapes=[pltpu.VMEM((HALO + TR, 768), F32)], compiler_params=_cp(1),
    )(z, z, mixp, pblk)


SCAN_LANE_TILES = 8


def _scan_rows(st, base, nrows, tab_ref, carry_ref, reverse):
    ng = nrows // 8
    edge = slice(0, 1) if reverse else slice(7, 8)
    for j0 in range(0, NST // 128, SCAN_LANE_TILES):
        cols = [(slice(j * 128, (j + 1) * 128), slice(NST + j * 128, NST + (j + 1) * 128))
                for j in range(j0, j0 + SCAN_LANE_TILES)]

        def group(gi, carry, cols=cols):
            g = (ng - 1 - gi) if reverse else gi
            r0 = pl.multiple_of(base + g * 8, 8)
            out = []
            for (cr, ci), (c_r, c_i) in zip(cols, carry):
                xr = st[pl.ds(r0, 8), cr]
                xi = st[pl.ds(r0, 8), ci]
                for k, sft in enumerate((1, 2, 4)):
                    ar, ai = tab_ref[8 * k:8 * k + 8, cr], tab_ref[8 * k:8 * k + 8, ci]
                    amt = (8 - sft) if reverse else sft
                    rr = pltpu.roll(xr, amt, 0)
                    ri = pltpu.roll(xi, amt, 0)
                    xr, xi = xr + (ar * rr - ai * ri), xi + (ar * ri + ai * rr)
                pr, pi = tab_ref[24:32, cr], tab_ref[24:32, ci]
                xr, xi = xr + (pr * c_r - pi * c_i), xi + (pr * c_i + pi * c_r)
                st[pl.ds(r0, 8), cr] = xr
                st[pl.ds(r0, 8), ci] = xi
                out.append((jnp.broadcast_to(xr[edge, :], (8, 128)), jnp.broadcast_to(xi[edge, :], (8, 128))))
            return tuple(out)

        res = lax.fori_loop(0, ng, group, tuple((carry_ref[:, cr], carry_ref[:, ci]) for cr, ci in cols))
        for (cr, ci), (c_r, c_i) in zip(cols, res):
            carry_ref[:, cr] = c_r
            carry_ref[:, ci] = c_i


def ssm_fwd(z, ymix, bblk, cblk, tabf, ssmv, wglu):
    s = z.shape[0]
    nt = s // TR

    def body(u_ref, ymix_in, bblk_ref, cblk_ref, tab_ref, v_ref, wglu_ref, yd_ref, y_ref, xb_ref, st, carry):
        i = pl.program_id(0)

        @pl.when(i == 0)
        def _():
            carry[...] = jnp.zeros_like(carry)

        xb_ref[...] = carry[...]
        u = u_ref[...]
        st[...] = _dot(u.astype(BF16), bblk_ref[...])
        _scan_rows(st, 0, TR, tab_ref, carry, reverse=False)
        y = _dot(st[...].astype(BF16), cblk_ref[...]) + v_ref[0:1, :] * u
        y_ref[...] = y
        yg, _ = _gelu(y)
        q = _dot(yg.astype(BF16), wglu_ref[...]) + v_ref[1:2, :]
        yd_ref[...] = (yg * _sigmoid(q)).astype(BF16)

    return pl.pallas_call(
        body, name="ssm_fwd", grid=(nt,),
        in_specs=[pl.BlockSpec((TR, GW), lambda i: (i, 6)), pl.BlockSpec(memory_space=pl.ANY),
                  _full((GW, 2 * NST)), _full((2 * NST, GW)), _full((32, 2 * NST)), _full((8, GW)), _full((GW, GW))],
        out_specs=[pl.BlockSpec((TR, GW), lambda i: (i, 3)), pl.BlockSpec((TR, GW), lambda i: (i, 0)),
                   pl.BlockSpec((8, 2 * NST), lambda i: (i, 0))],
        out_shape=[SDS((s, D), BF16), SDS((s, GW), F32), SDS((nt * 8, 2 * NST), F32)],
        scratch_shapes=[pltpu.VMEM((TR, 2 * NST), F32), pltpu.VMEM((8, 2 * NST), F32)],
        input_output_aliases={1: 0}, compiler_params=_cp(1),
    )(z, ymix, bblk, cblk, tabf, ssmv, wglu)


def mm_res_ln(a, w, xres, mod, lnp, g_row, ln_row, h2_rows):
    s, k = a.shape

    def body(a_ref, w_ref, x_ref, mod_ref, ln_ref, f_ref, r_ref, xo_ref, *h_ref):
        f = _dot(a_ref[...], w_ref[...])
        f_ref[...] = f
        r = ALPHA * x_ref[...] + (1.0 + mod_ref[g_row:g_row + 1, :]) * f
        r_ref[...] = r
        xhat, _ = _ln_stats(r)
        xo = xhat * ln_ref[ln_row:ln_row + 1, :] + ln_ref[ln_row + 1:ln_row + 2, :]
        xo_ref[...] = xo
        if h2_rows is not None:
            sh, sc = h2_rows
            h_ref[0][...] = (xo * (1.0 + mod_ref[sc:sc + 1, :]) + mod_ref[sh:sh + 1, :]).astype(BF16)

    row = pl.BlockSpec((TR, D), lambda i: (i, 0))
    outs = [SDS((s, D), F32)] * 3 + ([SDS((s, D), BF16)] if h2_rows is not None else [])
    return pl.pallas_call(
        body, name="mm_res_ln_%d" % k, grid=(s // TR,),
        in_specs=[pl.BlockSpec((TR, k), lambda i: (i, 0)), _full((k, D)), row, _full((8, D)), _full((8, D))],
        out_specs=[row] * len(outs), out_shape=outs, compiler_params=_cp(1),
    )(a, w, xres, mod, lnp)


def mm_gate_up(h2, wg_t, wu_t, carry=()):
    s = h2.shape[0]
    nn, nt = D_FF // FF_BLK, s // TR
    na = len(carry)

    def body(*refs):
        h_ref, wg_ref, wu_ref = refs[:3]
        gt_ref, up_ref, a_ref = refs[3 + na:6 + na]
        if na:
            step = pl.program_id(0) * nt + pl.program_id(1)
            start, forward, finish = _gather_phases(refs[3:3 + na], refs[6 + na:6 + 2 * na], *refs[6 + 2 * na:])
            pl.when(step == 0)(start)
        h = h_ref[...]
        gt = _dot_nt(h, wg_ref[...])
        up = _dot_nt(h, wu_ref[...])
        gt_ref[...] = gt
        up_ref[...] = up
        a_ref[...] = (gt * _sigmoid(gt) * up).astype(BF16)
        if na:
            pl.when(step == nn * nt - 1 - max(1, nt // 4))(forward)
            pl.when(step == nn * nt - 1)(finish)

    wspec = pl.BlockSpec((FF_BLK, D), lambda n, i: (n, 0))
    ospec = pl.BlockSpec((TR, FF_BLK), lambda n, i: (i, n))
    sems = [pltpu.SemaphoreType.DMA((7 * na,)), pltpu.SemaphoreType.DMA((7 * na,)), pltpu.SemaphoreType.DMA((na,))]
    return pl.pallas_call(
        body, name="mm_gate_up_carry" if na else "mm_gate_up", grid=(nn, nt),
        in_specs=[pl.BlockSpec((TR, D), lambda n, i: (i, 0)), wspec, wspec] + [ANY] * na,
        out_specs=[ospec, ospec, ospec] + [ANY] * na,
        out_shape=[SDS((s, D_FF), F32), SDS((s, D_FF), F32), SDS((s, D_FF), BF16)]
        + [SDS((N_DEV * v.shape[0], v.shape[1]), v.dtype) for v in carry],
        scratch_shapes=sems if na else [], compiler_params=_cp(2),
    )(h2, wg_t, wu_t, *carry)


def loss_fwd(x, target):
    s = x.shape[0]

    def body(x_ref, t_ref, l_ref, dx_ref):
        i = pl.program_id(0)

        @pl.when(i == 0)
        def _():
            l_ref[...] = jnp.zeros_like(l_ref)

        e = x_ref[...] - t_ref[...]
        dx_ref[...] = e * (1.0 / D)
        part = jnp.sum(jnp.mean(e * e, axis=-1, keepdims=True), axis=0, keepdims=True)
        l_ref[...] = l_ref[...] + 0.5 * part

    row = pl.BlockSpec((TR, D), lambda i: (i, 0))
    return pl.pallas_call(
        body, name="loss_fwd", grid=(s // TR,), in_specs=[row, row], out_specs=[_full((8, 128)), row],
        out_shape=[SDS((8, 128), F32), SDS((s, D), F32)], compiler_params=_cp(1),
    )(x, target)


def bwd_ln_o(dx, r, y, w_o, mod, lnp):
    s = dx.shape[0]

    def body(dx_ref, r_ref, y_ref, w_ref, mod_ref, ln_ref, dr_ref, dy_ref, dm_ref, st_ref):
        i = pl.program_id(0)

        @pl.when(i == 0)
        def _():
            st_ref[...] = jnp.zeros_like(st_ref)

        dx = dx_ref[...]
        dr, dg = _ln_bwd(dx, r_ref[...], ln_ref[0:1, :])
        dr_ref[...] = dr
        dy = ((1.0 + mod_ref[2:3, :]) * dr).astype(BF16)
        dy_ref[...] = dy
        dm_ref[...] = _dot_nt(dy, w_ref[...])
        st_ref[0:1, :] += _rowsum(dr * y_ref[...])
        st_ref[1:2, :] += _rowsum(dg)
        st_ref[2:3, :] += _rowsum(dx)

    row = pl.BlockSpec((TR, D), lambda i: (i, 0))
    return pl.pallas_call(
        body, name="bwd_ln_o", grid=(s // TR,),
        in_specs=[row, row, row, _full((D, D)), _full((8, D)), _full((8, D))],
        out_specs=[row, row, row, _full((8, D))],
        out_shape=[SDS((s, D), F32), SDS((s, D), BF16), SDS((s, D), F32), SDS((8, D), F32)], compiler_params=_cp(1),
    )(dx, r, y, w_o, mod, lnp)


def bwd_ln_down(dx, r, f, gt, up, w_down, mod, lnp, carry=()):
    s = dx.shape[0]
    nn, nt = D_FF // FF_BLK, s // TR
    na = len(carry)

    def body(*refs):
        dx_ref, r_ref, f_ref, gt_ref, up_ref, w_ref, mod_ref, ln_ref = refs[:8]
        dr_ref, df_ref, dgt_ref, dup_ref, st_ref = refs[8 + na:13 + na]
        i = pl.program_id(0)
        n = pl.program_id(1)
        if na:
            start, finish = _scatter_phases(refs[8:8 + na], refs[13 + na:13 + 2 * na], *refs[13 + 2 * na:])
            pl.when((i == 0) & (n == 0))(start)

        @pl.when((i == 0) & (n == 0))
        def _():
            st_ref[...] = jnp.zeros_like(st_ref)

        @pl.when(n == 0)
        def _():
            dx = dx_ref[...]
            dr, dg = _ln_bwd(dx, r_ref[...], ln_ref[2:3, :])
            dr_ref[...] = dr
            df_ref[...] = ((1.0 + mod_ref[5:6, :]) * dr).astype(BF16)
            st_ref[0:1, :] += _rowsum(dr * f_ref[...])
            st_ref[1:2, :] += _rowsum(dg)
            st_ref[2:3, :] += _rowsum(dx)

        da = _dot_nt(df_ref[...], w_ref[...])
        gt = gt_ref[...]
        sg = _sigmoid(gt)
        dgt_ref[...] = (da * up_ref[...] * (sg * (1.0 + gt * (1.0 - sg)))).astype(BF16)
        dup_ref[...] = (da * (gt * sg)).astype(BF16)
        if na:
            pl.when((i == nt - 1) & (n == nn - 1))(finish)

    row = pl.BlockSpec((TR, D), lambda i, n: (i, 0))
    ff = pl.BlockSpec((TR, FF_BLK), lambda i, n: (i, n))
    par = pl.BlockSpec((8, D), lambda i, n: (0, 0))
    sems = [pltpu.SemaphoreType.DMA((3 * na,)), pltpu.SemaphoreType.DMA((3 * na,))]
    return pl.pallas_call(
        body, name="bwd_ln_down_carry" if na else "bwd_ln_down", grid=(nt, nn),
        in_specs=[row, row, row, ff, ff, pl.BlockSpec((FF_BLK, D), lambda i, n: (n, 0)), par, par] + [ANY] * na,
        out_specs=[row, row, ff, ff, par] + [ANY] * na,
        out_shape=[SDS((s, D), F32), SDS((s, D), BF16), SDS((s, D_FF), BF16), SDS((s, D_FF), BF16), SDS((8, D), F32)]
        + [SDS((3,) + p.shape[1:], p.dtype) for p in carry],
        scratch_shapes=sems if na else [], compiler_params=_cp(2),
    )(dx, r, f, gt, up, w_down, mod, lnp, *carry)


def bwd_dx_mod(parts, dr, xin, mod, sh_row, sc_row, name):
    s = dr.shape[0]
    npart = len(parts)

    def body(*refs):
        ins, (dx_ref, st_ref) = refs[:2 * npart + 3], refs[2 * npart + 3:]
        dr_ref, x_ref, mod_ref = ins[2 * npart:]
        i = pl.program_id(0)

        @pl.when(i == 0)
        def _():
            st_ref[...] = jnp.zeros_like(st_ref)

        dh = _dot(ins[0][...], ins[1][...])
        for k in range(1, npart):
            dh = dh + _dot(ins[2 * k][...], ins[2 * k + 1][...])
        dx_ref[...] = ALPHA * dr_ref[...] + dh * (1.0 + mod_ref[sc_row:sc_row + 1, :])
        st_ref[0:1, :] += _rowsum(dh)
        st_ref[1:2, :] += _rowsum(dh * x_ref[...])

    row = pl.BlockSpec((TR, D), lambda i: (i, 0))
    in_specs, args = [], []
    for g, w in parts:
        kk = g.shape[1]
        in_specs += [pl.BlockSpec((TR, kk), lambda i: (i, 0)), _full((kk, D))]
        args += [g, w]
    return pl.pallas_call(
        body, name=name, grid=(s // TR,), in_specs=in_specs + [row, row, _full((8, D))],
        out_specs=[row, _full((8, D))], out_shape=[SDS((s, D), F32), SDS((8, D), F32)], compiler_params=_cp(1),
    )(*args, dr, xin, mod)


def mm_tn(a, b, tm, name):
    s, m = a.shape
    n = b.shape[1]
    tk = min(TK, s)

    def body(a_ref, b_ref, o_ref):
        @pl.when(pl.program_id(1) == 0)
        def _():
            o_ref[...] = jnp.zeros_like(o_ref)

        o_ref[...] += _dot_tn(a_ref[...], b_ref[...])

    return pl.pallas_call(
        body, name=name, grid=(m // tm, s // tk),
        in_specs=[pl.BlockSpec((tk, tm), lambda j, k: (k, j)), pl.BlockSpec((tk, n), lambda j, k: (k, 0))],
        out_specs=pl.BlockSpec((tm, n), lambda j, k: (j, 0)), out_shape=SDS((m, n), F32), compiler_params=_cp(2),
    )(a, b)


def mix_bwd(z, dymix, hc, mixp, pblk):
    s = z.shape[0]
    nt = s // TR
    r = TR // HALO
    ext = TR + HALO

    def body(z_ref, zp_ref, zn_ref, dy_ref, dyn_ref, hc_ref, hcn_ref, p_ref, pblk_ref,
             dz_ref, st_ref, dp_ref, db_ref, fbuf, bbuf, dpacc):
        i = pl.program_id(0)

        @pl.when(i == 0)
        def _():
            st_ref[...] = jnp.zeros_like(st_ref)
            dpacc[...] = jnp.zeros_like(dpacc)
            db_ref[...] = jnp.zeros_like(db_ref)

        z = z_ref[...]
        zp = jnp.where(i > 0, zp_ref[...], 0.0)
        dy = dy_ref[...]
        dyn = jnp.where(i < nt - 1, dyn_ref[...], 0.0)
        fbuf[0:HALO, 0:256] = zp[:, 512:768] * zp[:, 0:256]
        fbuf[HALO:ext, 0:256] = z[:, 512:768] * z[:, 0:256]
        fbuf[0:HALO, 256:512] = zp[:, 768:1024]
        fbuf[HALO:ext, 256:512] = z[:, 768:1024]
        fbuf[0:HALO, 512:768] = zp[:, 1024:1280] * _sigmoid(zp[:, 1280:1536])
        sg = _sigmoid(z[:, 1280:1536])
        fbuf[HALO:ext, 512:768] = z[:, 1024:1280] * sg

        cv = p_ref[0:1, :] * fbuf[pl.ds(HALO - 2, TR), 0:256]
        for j in range(1, SC_TAPS):
            cv = cv + p_ref[j:j + 1, :] * fbuf[pl.ds(HALO - 2 + j, TR), 0:256]
        dz_b = dy[:, 0:256] * cv
        dcv = dy[:, 0:256] * z[:, 256:512]
        bbuf[0:TR, 0:256] = dcv
        bbuf[TR:ext, 0:256] = dyn[:, 0:256] * zn_ref[...]
        da = p_ref[0:1, :] * bbuf[pl.ds(2, TR), 0:256]
        for j in range(1, SC_TAPS):
            da = da + p_ref[j:j + 1, :] * bbuf[pl.ds(2 - j, TR), 0:256]
        for j in range(SC_TAPS):
            st_ref[j:j + 1, :] += _rowsum(dcv * fbuf[pl.ds(HALO - 2 + j, TR), 0:256])
        dz_ref[:, 0:256] = (da * z[:, 512:768]).astype(BF16)
        dz_ref[:, 256:512] = dz_b.astype(BF16)
        dz_ref[:, 512:768] = (da * z[:, 0:256]).astype(BF16)
        db_ref[0:1, 0:256] += _rowsum(da * z[:, 512:768])
        db_ref[0:1, 256:512] += _rowsum(dz_b)
        db_ref[0:1, 512:768] += _rowsum(da * z[:, 0:256])

        ps = p_ref[MIXP_PS:MIXP_PS + 1, :]
        lane, cnt = _pool_consts(i, ext)
        acc = fbuf[pl.ds(HALO, TR), 256:512]
        caps = {}
        for k in range(1, 16):
            acc = acc + fbuf[pl.ds(HALO - k, TR), 256:512]
            if k + 1 in POOL_WINDOWS:
                caps[k + 1] = acc
        dmean = (_pick_window(lane, caps[2], caps[4], caps[8], caps[16]) / cnt[0:TR, :] - z[:, 768:1024]).astype(BF16)
        o = _dot(dmean, pblk_ref[...])
        st_ref[MIXP_PS:MIXP_PS + 1, :] += _rowsum(dy[:, 256:512] * o)
        do = (jnp.concatenate([dy[:, 256:512], dyn[:, 256:512]], axis=0) * ps).astype(BF16)
        dd = _dot_nt(do, pblk_ref[...])
        bbuf[:, 256:512] = dd / cnt
        dpacc[...] += _dot_tn(dmean, do[0:TR, :])
        acc = bbuf[pl.ds(0, TR), 256:512]
        caps = {}
        for k in range(1, 16):
            acc = acc + bbuf[pl.ds(k, TR), 256:512]
            if k + 1 in POOL_WINDOWS:
                caps[k + 1] = acc
        dz_p = _pick_window(lane, caps[2], caps[4], caps[8], caps[16]) - dd[0:TR, :]
        dz_ref[:, 768:1024] = dz_p.astype(BF16)
        db_ref[0:1, 768:1024] += _rowsum(dz_p)

        hce = jnp.concatenate([hc_ref[...], hcn_ref[...]], axis=0)
        dye = jnp.concatenate([dy[:, 512:768], dyn[:, 512:768]], axis=0)
        gam = p_ref[MIXP_LG:MIXP_LG + 1, :]
        xhat, rstd = _ln_stats(hce)
        hn = xhat * gam + p_ref[MIXP_LB:MIXP_LB + 1, :]
        sh = _sigmoid(hn)
        dhn = dye * (sh * (1.0 + hn * (1.0 - sh)))
        dxh = dhn * gam
        m1 = jnp.mean(dxh, axis=-1, keepdims=True)
        m2 = jnp.mean(dxh * xhat, axis=-1, keepdims=True)
        dhc = rstd * (dxh - m1 - xhat * m2)
        bbuf[:, 512:768] = dhc
        st_ref[MIXP_LG:MIXP_LG + 1, :] += _rowsum((dhn * xhat)[0:TR, :])
        st_ref[MIXP_LB:MIXP_LB + 1, :] += _rowsum(dhn[0:TR, :])
        dhc_t = dhc[0:TR, :]
        st_ref[MIXP_CB:MIXP_CB + 1, :] += _rowsum(dhc_t)
        dhg = p_ref[MIXP_CW:MIXP_CW + 1, :] * bbuf[pl.ds(30, TR), 512:768]
        st_ref[MIXP_CW:MIXP_CW + 1, :] += _rowsum(dhc_t * fbuf[pl.ds(HALO - 30, TR), 512:768])
        for j in range(1, CF_TAPS):
            dhg = dhg + p_ref[MIXP_CW + j:MIXP_CW + j + 1, :] * bbuf[pl.ds(30 - j, TR), 512:768]
            st_ref[MIXP_CW + j:MIXP_CW + j + 1, :] += _rowsum(dhc_t * fbuf[pl.ds(HALO - 30 + j, TR), 512:768])
        dz_v = dhg * sg
        dz_g = dhg * z[:, 1024:1280] * (sg * (1.0 - sg))
        dz_ref[:, 1024:1280] = dz_v.astype(BF16)
        dz_ref[:, 1280:1536] = dz_g.astype(BF16)
        db_ref[0:1, 1024:1280] += _rowsum(dz_v)
        db_ref[0:1, 1280:1536] += _rowsum(dz_g)

        @pl.when(i == nt - 1)
        def _():
            for k in range(4):
                dp_ref[64 * k:64 * k + 64, :] = dpacc[64 * k:64 * k + 64, 64 * k:64 * k + 64]

    nxt = lambda i: jnp.minimum((i + 1) * r, nt * r - 1)
    return pl.pallas_call(
        body, name="mix_bwd", grid=(nt,),
        in_specs=[pl.BlockSpec((TR, 1536), lambda i: (i, 0)),
                  pl.BlockSpec((HALO, 1536), lambda i: (jnp.maximum(i * r - 1, 0), 0)),
                  pl.BlockSpec((HALO, GW), lambda i: (nxt(i), 1)),
                  pl.BlockSpec((TR, 768), lambda i: (i, 0)), pl.BlockSpec((HALO, 768), lambda i: (nxt(i), 0)),
                  pl.BlockSpec((TR, GW), lambda i: (i, 0)), pl.BlockSpec((HALO, GW), lambda i: (nxt(i), 0)),
                  _full((MIXP_ROWS, GW)), _full((GW, GW))],
        out_specs=[pl.BlockSpec((TR, 1536), lambda i: (i, 0)), _full((MIXP_ROWS, GW)), _full((GW, 64)), _full((8, 1536))],
        out_shape=[SDS((s, IN_W), BF16), SDS((MIXP_ROWS, GW), F32), SDS((GW, 64), F32), SDS((8, 1536), F32)],
        scratch_shapes=[pltpu.VMEM((ext, 768), F32), pltpu.VMEM((ext, 768), F32), pltpu.VMEM((GW, GW), F32)],
        compiler_params=_cp(1),
    )(z, z, z, dymix, dymix, hc, hc, mixp, pblk)


def ssm_bwd(z, dz, dymix, y, xb, bblk, cblk, tabf, tabb, ssmv, wglu):
    s = z.shape[0]
    nt = s // TR

    def body(u_ref, dzin, dyd_ref, y_ref, xb_ref, bblk_ref, cblk_ref, tabf_ref, tabb_ref, v_ref, wglu_ref,
             dzs_ref, dbc_ref, dcc_ref, da_ref, dwg_ref, vst_ref, st, gs, carry, gcarry, dbacc, dcacc):
        i = pl.program_id(0)

        @pl.when(i == 0)
        def _():
            gcarry[...] = jnp.zeros_like(gcarry)
            for ref in (dbacc, dcacc, da_ref, dwg_ref, vst_ref):
                ref[...] = jnp.zeros_like(ref)

        u = u_ref[...]
        y = y_ref[...]
        yg, th = _gelu(y)
        ygb = yg.astype(BF16)
        q = _dot(ygb, wglu_ref[...]) + v_ref[1:2, :]
        sq = _sigmoid(q)
        dout = dyd_ref[...]
        dq = dout * yg * (sq * (1.0 - sq))
        dqb = dq.astype(BF16)
        dyg = dout * sq + _dot_nt(dqb, wglu_ref[...])
        dy = dyg * _gelu_grad(y, th)
        dyb = dy.astype(BF16)
        dwg_ref[...] += _dot_tn(ygb, dqb)
        vst_ref[0:1, :] += _rowsum(dy * u)
        vst_ref[1:2, :] += _rowsum(dq)
        ub = u.astype(BF16)
        carry[...] = xb_ref[...]
        st[0:8, :] = xb_ref[...]
        st[8:8 + TR, :] = _dot(ub, bblk_ref[...])
        _scan_rows(st, 8, TR, tabf_ref, carry, reverse=False)
        gs[...] = _dot_nt(dyb, cblk_ref[...])
        _scan_rows(gs, 0, TR, tabb_ref, gcarry, reverse=True)
        xs = st[pl.ds(8, TR), :]
        dcacc[...] += _dot_tn(dyb, xs.astype(BF16))
        g = gs[...]
        gb = g.astype(BF16)
        dbacc[...] += _dot_tn(ub, gb)
        gr, gi = g[:, 0:NST], g[:, NST:]
        xp = st[pl.ds(7, TR), :]
        xr, xi = xp[:, 0:NST], xp[:, NST:]
        da_ref[0:1, 0:NST] += _rowsum(gr * xr + gi * xi)
        da_ref[0:1, NST:] += _rowsum(gi * xr - gr * xi)
        du = _dot_nt(gb, bblk_ref[...]) + v_ref[0:1, :] * dy
        dzs_ref[...] = du.astype(BF16)
        vst_ref[2:3, :] += _rowsum(du)

        @pl.when(i == nt - 1)
        def _():
            for g_ in range(SSM_G):
                rows = slice(g_ * SSM_H, (g_ + 1) * SSM_H)
                for acc, out in ((dbacc, dbc_ref), (dcacc, dcc_ref)):
                    out[rows, 0:SSM_P] = acc[rows, g_ * SSM_P:(g_ + 1) * SSM_P]
                    out[rows, SSM_P:2 * SSM_P] = acc[rows, NST + g_ * SSM_P:NST + (g_ + 1) * SSM_P]

    rev = lambda i: nt - 1 - i
    return pl.pallas_call(
        body, name="ssm_bwd", grid=(nt,),
        in_specs=[pl.BlockSpec((TR, GW), lambda i: (rev(i), 6)), pl.BlockSpec(memory_space=pl.ANY),
                  pl.BlockSpec((TR, GW), lambda i: (rev(i), 3)), pl.BlockSpec((TR, GW), lambda i: (rev(i), 0)),
                  pl.BlockSpec((8, 2 * NST), lambda i: (rev(i), 0)),
                  _full((GW, 2 * NST)), _full((2 * NST, GW)), _full((32, 2 * NST)), _full((32, 2 * NST)),
                  _full((8, GW)), _full((GW, GW))],
        out_specs=[pl.BlockSpec((TR, GW), lambda i: (rev(i), 6)), _full((GW, 2 * SSM_P)), _full((GW, 2 * SSM_P)),
                   _full((8, 2 * NST)), _full((GW, GW)), _full((8, GW))],
        out_shape=[SDS((s, IN_W), BF16), SDS((GW, 2 * SSM_P), F32), SDS((GW, 2 * SSM_P), F32), SDS((8, 2 * NST), F32),
                   SDS((GW, GW), F32), SDS((8, GW), F32)],
        scratch_shapes=[pltpu.VMEM((8 + TR, 2 * NST), F32), pltpu.VMEM((TR, 2 * NST), F32),
                        pltpu.VMEM((8, 2 * NST), F32), pltpu.VMEM((8, 2 * NST), F32),
                        pltpu.VMEM((GW, 2 * NST), F32), pltpu.VMEM((GW, 2 * NST), F32)],
        input_output_aliases={1: 0}, compiler_params=_cp(1),
    )(z, dz, dymix, y, xb, bblk, cblk, tabf, tabb, ssmv, wglu)


def _ssm_prep(lam_re, lam_im, log_dt, b_re, b_im):
    dt = jnp.exp(log_dt)[..., None]
    mag = jnp.exp(lam_re * dt)
    ar, ai = mag * jnp.cos(lam_im * dt), mag * jnp.sin(lam_im * dt)
    den = lam_re * lam_re + lam_im * lam_im
    qr = ((ar - 1.0) * lam_re + ai * lam_im) / den
    qi = (ai * lam_re - (ar - 1.0) * lam_im) / den
    bbr = qr[..., None] * b_re - qi[..., None] * b_im
    bbi = qr[..., None] * b_im + qi[..., None] * b_re
    return ar, ai, bbr, bbi


def _ssm_tables(lam_re, lam_im, log_dt):
    nl = lam_re.shape[0]
    dt = jnp.exp(log_dt)[:, None, :, None]
    k = jnp.arange(1, 9, dtype=F32)[None, :, None, None]
    mag = jnp.exp(k * (lam_re[:, None] * dt))
    ang = k * (lam_im[:, None] * dt)
    pr = (mag * jnp.cos(ang)).reshape(nl, 8, NST)
    pi = (mag * jnp.sin(ang)).reshape(nl, 8, NST)
    row = jnp.arange(8)[None, :, None]

    def table(sign, reverse):
        parts_r, parts_i = [], []
        for sft in (1, 2, 4):
            keep = (row < 8 - sft) if reverse else (row >= sft)
            parts_r.append(jnp.where(keep, pr[:, sft - 1:sft], 0.0))
            parts_i.append(jnp.where(keep, sign * pi[:, sft - 1:sft], 0.0))
        parts_r.append(pr[:, ::-1] if reverse else pr)
        parts_i.append(sign * (pi[:, ::-1] if reverse else pi))
        return jnp.concatenate([jnp.concatenate(parts_r, axis=1), jnp.concatenate(parts_i, axis=1)], axis=2)

    return table(1.0, False), table(-1.0, True)


def _blockdiag(m):
    nl, g, a, b = m.shape
    return jnp.einsum('lgab,gk->lgakb', m, jnp.eye(g, dtype=m.dtype)).reshape(nl, g * a, g * b)


def _rows_at(blocks, total):
    out, at = [], 0
    nl, _, c = blocks[0][1].shape
    for r0, b in blocks:
        if r0 > at:
            out.append(jnp.zeros((nl, r0 - at, c), F32))
        out.append(b)
        at = r0 + b.shape[1]
    if total > at:
        out.append(jnp.zeros((nl, total - at, c), F32))
    return jnp.concatenate(out, axis=1)


def prep_params(p):
    ar, ai, bbr, bbi = _ssm_prep(p['ssm_lam_re'], p['ssm_lam_im'], p['ssm_log_dt'], p['ssm_b_re'], p['ssm_b_im'])
    bblk = jnp.concatenate([_blockdiag(jnp.swapaxes(bbr, 2, 3)), _blockdiag(jnp.swapaxes(bbi, 2, 3))], axis=2)
    cblk = jnp.concatenate([_blockdiag(jnp.swapaxes(p['ssm_c_re'], 2, 3)),
                            -_blockdiag(jnp.swapaxes(p['ssm_c_im'], 2, 3))], axis=1)
    tabf, tabb = _ssm_tables(p['ssm_lam_re'], p['ssm_lam_im'], p['ssm_log_dt'])
    mixp = _rows_at([(0, p['sc_w']), (MIXP_CW, p['cf_dw_w']), (MIXP_CB, p['cf_dw_b'][:, None]),
                     (MIXP_LG, p['cf_ln_g'][:, None]), (MIXP_LB, p['cf_ln_b'][:, None]),
                     (MIXP_PS, p['pool_scale'][:, None])], MIXP_ROWS)
    return dict(
        mixp=mixp, pblk=_blockdiag(p['pool_w']).astype(BF16), bblk=bblk.astype(BF16), cblk=cblk.astype(BF16),
        tabf=tabf, tabb=tabb, ssmv=_rows_at([(0, p['ssm_d'][:, None]), (1, p['ssm_b_glu'][:, None])], 8),
        lnp=_rows_at([(0, p['ln1_g'][:, None]), (1, p['ln1_b'][:, None]), (2, p['ln2_g'][:, None]),
                      (3, p['ln2_b'][:, None])], 8),
        b_in=p['b_in'][:, None],
    )


def layer_fwd(x, mod, w, q, next_halves=()):
    z, h1 = mm_in_fwd(x, mod, w['w_in'], q['b_in'])
    ymix, hc = mix_fwd(z, q['mixp'], q['pblk'])
    ymix, ys, xb = ssm_fwd(z, ymix, q['bblk'], q['cblk'], q['tabf'], q['ssmv'], w['ssm_w_glu'])
    y, r1, x1, h2 = mm_res_ln(ymix, w['w_o'], x, mod, q['lnp'], 2, 0, (3, 4))
    gt, up, act, *w_next = mm_gate_up(h2, w['w_gate'], w['w_up'], next_halves)
    f, r2, x2 = mm_res_ln(act, w['w_down'], x1, mod, q['lnp'], 5, 2, None)
    saved = dict(x=x, z=z, h1=h1, ymix=ymix, hc=hc, ys=ys, xb=xb, y=y, r1=r1, x1=x1, h2=h2, gt=gt, up=up, act=act, f=f, r2=r2)
    return x2, saved, w_next


def layer_bwd(dx2, mod, w, q, sv, pending=()):
    dr2, df, dgt, dup, st2, *recv = bwd_ln_down(dx2, sv['r2'], sv['f'], sv['gt'], sv['up'], w['w_down'], mod, q['lnp'],
                                                pending)
    dx1, stm2 = bwd_dx_mod([(dgt, w['w_gate']), (dup, w['w_up'])], dr2, sv['x1'], mod, 3, 4, "bwd_dx_ff")
    g_down = mm_tn(sv['act'], df, FF_BLK, "dw_down")
    g_gate = mm_tn(dgt, sv['h2'], FF_BLK, "dw_gate")
    g_up = mm_tn(dup, sv['h2'], FF_BLK, "dw_up")
    dr1, dy, dymix, st1 = bwd_ln_o(dx1, sv['r1'], sv['y'], w['w_o'], mod, q['lnp'])
    g_o = mm_tn(sv['ymix'], dy, D // 2, "dw_o")
    dz, mst, dpool, dbin = mix_bwd(sv['z'], dymix, sv['hc'], q['mixp'], q['pblk'])
    dz, dbc, dcc, da, g_glu, vst = ssm_bwd(sv['z'], dz, dymix, sv['ys'], sv['xb'], q['bblk'], q['cblk'], q['tabf'],
                                           q['tabb'], q['ssmv'], w['ssm_w_glu'])
    dx, stm1 = bwd_dx_mod([(dz, w['w_in'])], dr1, sv['x'], mod, 0, 1, "bwd_dx_in")
    g_in = mm_tn(dz, sv['h1'], IN_W // 2, "dw_in")
    big = [g_in, g_o, g_gate, g_up, g_down, g_glu]
    stats = dict(s1024=jnp.concatenate([stm1, st1, stm2, st2], axis=0), mst=mst, vst=vst, dpool=dpool,
                 dbc=dbc, dcc=dcc, da=da, dbin=dbin)
    return dx, big, stats, recv


MESH = pl.DeviceIdType.MESH
ANY = pl.BlockSpec(memory_space=pl.ANY)
VMEM_SPEC = pl.BlockSpec(memory_space=pltpu.VMEM)


def _place():
    return lax.axis_index("x"), lax.axis_index("y"), lax.axis_index("c")


def _gather_phases(x_refs, out_refs, send_sems, recv_sems, local_sems):
    na = len(x_refs)

    def parts():
        x, y, c = _place()
        chips = [(1 - x, y), (x, 1 - y), (1 - x, 1 - y)]

        def rows(a, px, py, pc):
            m_per = x_refs[a].shape[0]
            return out_refs[a].at[pl.ds((4 * px + 2 * py + pc) * m_per, m_per), :]

        def copy(a, k, block, to, src=None):
            return pltpu.make_async_remote_copy(
                src_ref=rows(a, *block) if src is None else src, dst_ref=rows(a, *block),
                send_sem=send_sems.at[7 * a + k], recv_sem=recv_sems.at[7 * a + k], device_id=to, device_id_type=MESH)

        def mine():
            return [pltpu.make_async_copy(x_refs[a], rows(a, x, y, c), local_sems.at[a]) for a in range(na)]

        def first():
            out = []
            for a in range(na):
                out.append(copy(a, 0, (x, y, c), (x, y, 1 - c), src=x_refs[a]))
                out += [copy(a, 1 + j, (x, y, c), (*chip, c), src=x_refs[a]) for j, chip in enumerate(chips)]
            return out

        def passed(j, a):
            return copy(a, 4 + j, (*chips[j], c), (x, y, 1 - c))

        return (x, y, c), chips, copy, mine, first, passed

    def start():
        _, _, _, mine, first, _ = parts()
        for cp in mine() + first():
            cp.start()

    def forward():
        me, chips, copy, _, _, passed = parts()
        for j, chip in enumerate(chips):
            for a in range(na):
                copy(a, 1 + j, (*chip, me[2]), me).wait_recv()
                passed(j, a).start()

    def finish():
        me, chips, copy, mine, first, passed = parts()
        for a in range(na):
            copy(a, 0, (me[0], me[1], 1 - me[2]), me).wait_recv()
            for j, chip in enumerate(chips):
                copy(a, 4 + j, (*chip, 1 - me[2]), me).wait_recv()
        for cp in first() + [passed(j, a) for j in range(3) for a in range(na)]:
            cp.wait_send()
        for cp in mine():
            cp.wait()

    return start, forward, finish


def _scatter_phases(ps_refs, out_refs, send_sems, recv_sems):
    na = len(ps_refs)

    def copies():
        x, y, c = _place()
        chips = [(1 - x, y), (x, 1 - y), (1 - x, 1 - y)]
        return [pltpu.make_async_remote_copy(
            src_ref=ps_refs[a].at[2 * kx + ky], dst_ref=out_refs[a].at[j], send_sem=send_sems.at[3 * a + j],
            recv_sem=recv_sems.at[3 * a + j], device_id=(kx, ky, c), device_id_type=MESH)
            for a in range(na) for j, (kx, ky) in enumerate(chips)]

    def start():
        for cp in copies():
            cp.start()

    def finish():
        for cp in copies():
            cp.wait()

    return start, finish


def allgather8(vs, name):
    na = len(vs)

    def body(*refs):
        start, forward, finish = _gather_phases(refs[:na], refs[na:2 * na], *refs[2 * na:])
        start()
        forward()
        finish()

    return pl.pallas_call(
        body, name=name, out_shape=[SDS((N_DEV * v.shape[0], v.shape[1]), v.dtype) for v in vs],
        in_specs=[VMEM_SPEC] * na, out_specs=[VMEM_SPEC] * na,
        scratch_shapes=[pltpu.SemaphoreType.DMA((7 * na,)), pltpu.SemaphoreType.DMA((7 * na,)),
                        pltpu.SemaphoreType.DMA((na,))],
        compiler_params=pltpu.CompilerParams(vmem_limit_bytes=VMEM_LIMIT),
    )(*vs)


def pair_swap(gs):
    na = len(gs)

    def body(*refs):
        g_refs, out_refs, (send_sems, recv_sems) = refs[:na], refs[na:2 * na], refs[2 * na:]
        x, y, c = _place()
        cps = [pltpu.make_async_remote_copy(
            src_ref=g_refs[a].at[j, 1 - c], dst_ref=out_refs[a].at[j], send_sem=send_sems.at[4 * a + j],
            recv_sem=recv_sems.at[4 * a + j], device_id=(x, y, 1 - c), device_id_type=MESH)
            for a in range(na) for j in range(N_CHIPS)]
        for cp in cps:
            cp.start()
        for cp in cps:
            cp.wait()

    return pl.pallas_call(
        body, name="pair_swap", out_shape=[SDS((g.shape[0],) + g.shape[2:], g.dtype) for g in gs],
        in_specs=[ANY] * na, out_specs=[ANY] * na,
        scratch_shapes=[pltpu.SemaphoreType.DMA((4 * na,)), pltpu.SemaphoreType.DMA((4 * na,))],
    )(*gs)


def chip_scatter(pss):
    na = len(pss)

    def body(*refs):
        start, finish = _scatter_phases(refs[:na], refs[na:2 * na], *refs[2 * na:])
        start()
        finish()

    return pl.pallas_call(
        body, name="chip_scatter", out_shape=[SDS((3,) + p.shape[1:], p.dtype) for p in pss],
        in_specs=[ANY] * na, out_specs=[ANY] * na,
        scratch_shapes=[pltpu.SemaphoreType.DMA((3 * na,)), pltpu.SemaphoreType.DMA((3 * na,))],
    )(*pss)


def pair_gather(ts):
    na = len(ts)

    def body(*refs):
        t_refs, out_refs, (send_sems, recv_sems) = refs[:na], refs[na:2 * na], refs[2 * na:]
        x, y, c = _place()
        cps = [pltpu.make_async_remote_copy(
            src_ref=t_refs[a].at[c], dst_ref=out_refs[a].at[c], send_sem=send_sems.at[a], recv_sem=recv_sems.at[a],
            device_id=(x, y, 1 - c), device_id_type=MESH) for a in range(na)]
        for cp in cps:
            cp.start()
        for cp in cps:
            cp.wait()

    return pl.pallas_call(
        body, name="pair_gather", out_shape=[SDS(t.shape, t.dtype) for t in ts],
        in_specs=[ANY] * na, out_specs=[ANY] * na, input_output_aliases={a: a for a in range(na)},
        scratch_shapes=[pltpu.SemaphoreType.DMA((na,)), pltpu.SemaphoreType.DMA((na,))],
    )(*ts)


def _scalar(v):
    return jnp.reshape(v, (1,)).astype(jnp.int32)


def pair_sum(gs, recvs):
    na = len(gs)
    c = lax.axis_index("c")

    def body(c_ref, *refs):
        for a in range(na):
            refs[2 * na + a][...] = (refs[a][...] + refs[na + a][...]).astype(BF16)

    in_specs = [pl.BlockSpec((None, None) + g.shape[2:], lambda j, c_ref: (j, c_ref[0], 0, 0)) for g in gs]
    in_specs += [pl.BlockSpec((None,) + r.shape[1:], lambda j, c_ref: (j, 0, 0)) for r in recvs]
    return pl.pallas_call(
        body, name="pair_sum",
        grid_spec=pltpu.PrefetchScalarGridSpec(
            num_scalar_prefetch=1, grid=(N_CHIPS,), in_specs=in_specs,
            out_specs=[pl.BlockSpec((None,) + r.shape[1:], lambda j, c_ref: (j, 0, 0)) for r in recvs]),
        out_shape=[SDS(r.shape, BF16) for r in recvs], compiler_params=_cp(1),
    )(_scalar(c), *gs, *recvs)


def sum_chips(pss, recvs):
    na = len(pss)
    xi, yi, ci = _place()

    def body(s_ref, *refs):
        for a in range(na):
            r = refs[na + a]
            acc = refs[a][...].astype(F32) + r[0].astype(F32)
            acc = acc + r[1].astype(F32)
            refs[2 * na + a][...] = acc + r[2].astype(F32)

    in_specs = [pl.BlockSpec((None,) + p.shape[1:], lambda i, s_ref: (s_ref[0], 0, 0)) for p in pss]
    in_specs += [pl.BlockSpec(r.shape, lambda i, s_ref: (0, 0, 0)) for r in recvs]
    return pl.pallas_call(
        body, name="sum_chips",
        grid_spec=pltpu.PrefetchScalarGridSpec(
            num_scalar_prefetch=1, grid=(1,), in_specs=in_specs,
            out_specs=[pl.BlockSpec((None,) + p.shape[1:], lambda i, s_ref: (s_ref[1], 0, 0)) for p in pss]),
        out_shape=[SDS((2,) + p.shape[1:], F32) for p in pss], compiler_params=_cp(1),
    )(jnp.stack([2 * xi + yi, ci]).astype(jnp.int32), *pss, *recvs)


def sum8(vs, rows):
    na = len(vs)

    def body(*refs):
        for a in range(na):
            m = rows[a]
            acc = refs[a][0:m, :]
            for d in range(1, N_DEV):
                acc = acc + refs[a][d * m:(d + 1) * m, :]
            refs[na + a][...] = acc

    return pl.pallas_call(
        body, name="sum8", out_shape=[SDS((rows[a], vs[a].shape[1]), F32) for a in range(na)],
        in_specs=[VMEM_SPEC] * na, out_specs=[VMEM_SPEC] * na,
        compiler_params=pltpu.CompilerParams(vmem_limit_bytes=VMEM_LIMIT),
    )(*vs)


def ada_fwd(c_all, w_ada):
    nl, _, n = w_ada.shape
    bn = 512

    def body(c_ref, w_ref, o_ref):
        cv = c_ref[...]
        cond = (cv * _sigmoid(cv)).astype(BF16)
        o_ref[...] = _dot(cond, w_ref[...].astype(BF16))

    return pl.pallas_call(
        body, name="ada_fwd", grid=(nl, n // bn),
        in_specs=[pl.BlockSpec((8, D), lambda l, j: (0, 0)), pl.BlockSpec((None, D, bn), lambda l, j: (l, 0, j))],
        out_specs=pl.BlockSpec((None, 8, bn), lambda l, j: (l, 0, j)), out_shape=SDS((nl, 8, n), F32),
        compiler_params=_cp(2),
    )(c_all, w_ada)


def ada_grad(c_all, dm):
    nl, _, n = dm.shape
    bn = 512

    def body(c_ref, d_ref, o_ref):
        cv = c_ref[...]
        cond = (cv * _sigmoid(cv)).astype(BF16)
        o_ref[...] = _dot_tn(cond, d_ref[...].astype(BF16))

    return pl.pallas_call(
        body, name="ada_grad", grid=(nl, n // bn),
        in_specs=[pl.BlockSpec((8, D), lambda l, j: (0, 0)), pl.BlockSpec((None, 8, bn), lambda l, j: (l, 0, j))],
        out_specs=pl.BlockSpec((None, D, bn), lambda l, j: (l, 0, j)), out_shape=SDS((nl, D, n), F32),
        compiler_params=_cp(2),
    )(c_all, dm)


def _adam_math(w, g, m, v):
    m = ADAM_B1 * m + (1.0 - ADAM_B1) * g
    v = ADAM_B2 * v + (1.0 - ADAM_B2) * (g * g)
    m_hat = m / (1.0 - ADAM_B1 ** ADAM_STEP)
    v_hat = v / (1.0 - ADAM_B2 ** ADAM_STEP)
    return -ADAM_LR * (m_hat / (jnp.sqrt(v_hat) + ADAM_EPS) + ADAM_WD * w), m, v


def adamw(w, g, m, v, name):
    rr, cc = w.shape
    tr = 256

    def body(w_ref, g_ref, m_ref, v_ref, d_ref, mo_ref, vo_ref):
        d_ref[...], mo_ref[...], vo_ref[...] = _adam_math(w_ref[...], g_ref[...], m_ref[...], v_ref[...])

    spec = pl.BlockSpec((tr, cc), lambda i: (i, 0))
    return pl.pallas_call(
        body, name=name, grid=(rr // tr,), in_specs=[spec] * 4, out_specs=[spec] * 3,
        out_shape=[SDS((rr, cc), F32)] * 3, compiler_params=_cp(1),
    )(w, g, m, v)


def adamw_many(ws, gs, ms, vs):
    na = len(ws)

    def body(*refs):
        for a in range(na):
            d, m, v = _adam_math(refs[a][...], refs[na + a][...], refs[2 * na + a][...], refs[3 * na + a][...])
            refs[4 * na + a][...] = d
            refs[5 * na + a][...] = m
            refs[6 * na + a][...] = v

    outs = [SDS(w.shape, F32) for w in ws] * 3
    res = pl.pallas_call(
        body, name="adamw_small", out_shape=outs, in_specs=[VMEM_SPEC] * (4 * na), out_specs=[VMEM_SPEC] * (3 * na),
        compiler_params=pltpu.CompilerParams(vmem_limit_bytes=VMEM_LIMIT),
    )(*ws, *gs, *ms, *vs)
    return res[:na], res[na:2 * na], res[2 * na:]


BIG = ('w_in', 'w_o', 'w_gate', 'w_up', 'w_down', 'ssm_w_glu')
BIG_T = ('w_in', 'w_gate', 'w_up')
WEIGHTS = ['w_ada', 'b_ada', 'w_in', 'b_in', 'sc_w', 'pool_w', 'pool_scale', 'cf_dw_w', 'cf_dw_b', 'cf_ln_g', 'cf_ln_b',
           'ssm_lam_re', 'ssm_lam_im', 'ssm_log_dt', 'ssm_b_re', 'ssm_b_im', 'ssm_c_re', 'ssm_c_im', 'ssm_d', 'ssm_w_glu',
           'ssm_b_glu', 'w_o', 'ln1_g', 'ln1_b', 'w_gate', 'w_up', 'w_down', 'ln2_g', 'ln2_b']
STAT_KINDS = ('s1024', 'mst', 'vst', 'dpool', 'dbc', 'dcc', 'da', 'dbin')


def _chip_blocks(g, width):
    m = g.shape[0] // N_DEV
    return g.reshape(N_CHIPS, 2, m, g.shape[1])[:, 0, :, :width]


def kernel(x, c, w_ada, b_ada, w_in, b_in, sc_w, pool_w, pool_scale, cf_dw_w, cf_dw_b, cf_ln_g, cf_ln_b, ssm_lam_re, ssm_lam_im, ssm_log_dt, ssm_b_re, ssm_b_im, ssm_c_re, ssm_c_im, ssm_d, ssm_w_glu, ssm_b_glu, w_o, ln1_g, ln1_b, w_gate, w_up, w_down, ln2_g, ln2_b, loss_target, m_w_ada, m_b_ada, m_w_in, m_b_in, m_sc_w, m_pool_w, m_pool_scale, m_cf_dw_w, m_cf_dw_b, m_cf_ln_g, m_cf_ln_b, m_ssm_lam_re, m_ssm_lam_im, m_ssm_log_dt, m_ssm_b_re, m_ssm_b_im, m_ssm_c_re, m_ssm_c_im, m_ssm_d, m_ssm_w_glu, m_ssm_b_glu, m_w_o, m_ln1_g, m_ln1_b, m_w_gate, m_w_up, m_w_down, m_ln2_g, m_ln2_b, v_w_ada, v_b_ada, v_w_in, v_b_in, v_sc_w, v_pool_w, v_pool_scale, v_cf_dw_w, v_cf_dw_b, v_cf_ln_g, v_cf_ln_b, v_ssm_lam_re, v_ssm_lam_im, v_ssm_log_dt, v_ssm_b_re, v_ssm_b_im, v_ssm_c_re, v_ssm_c_im, v_ssm_d, v_ssm_w_glu, v_ssm_b_glu, v_w_o, v_ln1_g, v_ln1_b, v_w_gate, v_w_up, v_w_down, v_ln2_g, v_ln2_b):
    a = dict(locals())
    W = {n: a[n] for n in WEIGHTS}
    M = {n: a['m_' + n] for n in WEIGHTS}
    V = {n: a['v_' + n] for n in WEIGHTS}
    xi, yi, ci = _place()
    me = 4 * xi + 2 * yi + ci
    chip = 2 * xi + yi
    nl = DEPTH

    g_c, g_sc, g_cf = allgather8(
        [jnp.pad(c, ((0, 7), (0, 0))), jnp.pad(sc_w.reshape(nl * SC_TAPS, 64), ((0, 4), (0, 0))),
         jnp.pad(cf_dw_w.reshape(nl * CF_TAPS, 64), ((0, 4), (0, 0)))], "gather_pre")
    c_all = g_c.reshape(N_DEV, 8, D)[:, 0]
    sc_full = jnp.moveaxis(_chip_blocks(g_sc, 64)[:, :nl * SC_TAPS].reshape(N_CHIPS, nl, SC_TAPS, 64), 0, 2)
    sc_full = sc_full.reshape(nl, SC_TAPS, GW)
    cf_full = jnp.moveaxis(_chip_blocks(g_cf, 64)[:, :nl * CF_TAPS].reshape(N_CHIPS, nl, CF_TAPS, 64), 0, 2)
    cf_full = cf_full.reshape(nl, CF_TAPS, GW)

    mod_part = ada_fwd(c_all, w_ada)
    (g_mod,) = allgather8([mod_part.reshape(nl * 8, -1)], "gather_mod")
    mod_all = jnp.moveaxis(_chip_blocks(g_mod, 6 * D // N_CHIPS).reshape(N_CHIPS, nl, 8, -1), 0, 2)
    mod_all = mod_all.reshape(nl, 8, 6 * D) + b_ada[:, None, :]
    mod_mine = lax.dynamic_index_in_dim(mod_all, me, axis=1, keepdims=False).reshape(nl, 6, D)
    mods = jnp.pad(mod_mine, ((0, 0), (0, 2), (0, 0)))

    halves = {}
    for n in BIG:
        wt = jnp.swapaxes(W[n], 1, 2) if n in BIG_T else W[n]
        hr = wt.shape[1] // 2
        halves[n] = lax.dynamic_slice_in_dim(wt, ci * hr, hr, axis=1).astype(BF16)
    half_l = [[halves[n][l] for n in BIG] for l in range(nl)]
    wfull = [dict(zip(BIG, allgather8(half_l[0], "gather_w")))]

    small_names = [n for n in WEIGHTS if n not in BIG and n not in ('w_ada', 'b_ada')]
    pfull = {n: W[n] for n in small_names}
    pfull['sc_w'], pfull['cf_dw_w'] = sc_full, cf_full
    Q = prep_params(pfull)
    h = x[0]
    saved = []
    for l in range(nl):
        h, sv, w_next = layer_fwd(h, mods[l], wfull[l], {k: v[l] for k, v in Q.items()},
                                  half_l[l + 1] if l + 1 < nl else ())
        saved.append(sv)
        if w_next:
            wfull.append(dict(zip(BIG, w_next)))
    l8, dh = loss_fwd(h, loss_target[0])
    loss = lax.psum(l8[0, 0], ("x", "y", "c"))

    stats = [None] * nl
    gbig = {n: [None] * nl for n in BIG}

    def finish_reduce(l, pss, recv):
        for n, t in zip(BIG, pair_gather(sum_chips(pss, recv))):
            gbig[n][l] = t.reshape(2 * t.shape[1], t.shape[2])

    pending = ()
    for l in reversed(range(nl)):
        dh, big, stats[l], recv = layer_bwd(dh, mods[l], wfull[l], {k: v[l] for k, v in Q.items()}, saved[l], pending)
        if pending:
            finish_reduce(l + 1, pending, recv)
        views = [g.reshape(N_CHIPS, 2, g.shape[0] // (2 * N_CHIPS), g.shape[1]) for g in big]
        pending = pair_sum(views, pair_swap(views))
    finish_reduce(0, pending, chip_scatter(pending))
    grad_x = dh[None]

    mine = [jnp.concatenate([stats[l][k] for l in range(nl)], axis=0) for k in STAT_KINDS]
    rows = [v.shape[0] for v in mine]
    gathered = allgather8(mine, "gather_small")
    S = dict(zip(STAT_KINDS, [v.reshape(nl, r // nl, v.shape[1]) for v, r in zip(sum8(gathered, rows), rows)]))
    G = {}
    s1 = S['s1024']
    mod_rows = (0, 1, 8, 16, 17, 24)
    G['b_ada'] = jnp.concatenate([s1[:, r] for r in mod_rows], axis=1)
    G['ln1_g'], G['ln1_b'], G['ln2_g'], G['ln2_b'] = s1[:, 9], s1[:, 10], s1[:, 25], s1[:, 26]
    mst = S['mst']
    sc_g, cf_g = mst[:, 0:SC_TAPS], mst[:, MIXP_CW:MIXP_CW + CF_TAPS]
    G['sc_w'] = lax.dynamic_slice_in_dim(sc_g, chip * 64, 64, axis=2)
    G['cf_dw_w'] = lax.dynamic_slice_in_dim(cf_g, chip * 64, 64, axis=2)
    G['cf_dw_b'], G['cf_ln_g'], G['cf_ln_b'], G['pool_scale'] = mst[:, MIXP_CB], mst[:, MIXP_LG], mst[:, MIXP_LB], mst[:, MIXP_PS]
    G['pool_w'] = S['dpool'].reshape(nl, 4, 64, 64)
    G['ssm_d'], G['ssm_b_glu'] = S['vst'][:, 0], S['vst'][:, 1]
    G['b_in'] = jnp.concatenate([S['dbin'][:, 0], S['vst'][:, 2]], axis=1)
    dbc = S['dbc'].reshape(nl, SSM_G, SSM_H, 2, SSM_P)
    dcc = S['dcc'].reshape(nl, SSM_G, SSM_H, 2, SSM_P)
    G['ssm_c_re'], G['ssm_c_im'] = dcc[:, :, :, 0], -dcc[:, :, :, 1]
    da = S['da'][:, 0]
    cot = (da[:, :NST].reshape(nl, SSM_G, SSM_P), da[:, NST:].reshape(nl, SSM_G, SSM_P),
           jnp.swapaxes(dbc[:, :, :, 0], 2, 3), jnp.swapaxes(dbc[:, :, :, 1], 2, 3))
    ssm_in = ('ssm_lam_re', 'ssm_lam_im', 'ssm_log_dt', 'ssm_b_re', 'ssm_b_im')
    _, vjp = jax.vjp(_ssm_prep, *[W[n] for n in ssm_in])
    G.update(dict(zip(ssm_in, vjp(cot))))
    g1 = gathered[0].reshape(N_DEV, nl, 32, D)
    dmod_all = jnp.concatenate([g1[:, :, r] for r in mod_rows], axis=2)
    dm_cols = lax.dynamic_slice_in_dim(jnp.swapaxes(dmod_all, 0, 1), chip * (6 * D // N_CHIPS), 6 * D // N_CHIPS, axis=2)
    G['w_ada'] = ada_grad(c_all, dm_cols)
    for n in BIG:
        g = jnp.stack(gbig[n])
        G[n] = jnp.swapaxes(g, 1, 2) if n in BIG_T else g

    big_names = ('w_ada',) + BIG
    delta, new_m, new_v = {}, {}, {}
    for n in big_names:
        shp = W[n].shape
        two = (shp[0] * shp[1], shp[2])
        d_, m_, v_ = adamw(W[n].reshape(two), G[n].reshape(two), M[n].reshape(two), V[n].reshape(two), "adamw_" + n)
        delta[n], new_m[n], new_v[n] = d_.reshape(shp), m_.reshape(shp), v_.reshape(shp)
    rest = [n for n in WEIGHTS if n not in big_names]
    for n in rest:
        G[n] = G[n].reshape(W[n].shape)
    lane_view = lambda t: t.reshape(t.shape[:-2] + (-1,)) if t.shape[-1] < 64 and t.ndim == 4 else t
    ds, ms, vs_ = adamw_many(*[[lane_view(t[n]) for n in rest] for t in (W, G, M, V)])
    for n, d_, m_, v_ in zip(rest, ds, ms, vs_):
        delta[n], new_m[n], new_v[n] = d_.reshape(W[n].shape), m_.reshape(W[n].shape), v_.reshape(W[n].shape)
    return (loss, grad_x, *[G[n] for n in WEIGHTS], *[delta[n] for n in WEIGHTS],
            *[new_m[n] for n in WEIGHTS], *[new_v[n] for n in WEIGHTS])
```

```python
import math

import jax
import jax.numpy as jnp
import numpy as np
from jax import lax
from jax.experimental import pallas as pl
from jax.experimental.pallas import tpu as pltpu

F32 = jnp.float32
BF16 = jnp.bfloat16
SDS = jax.ShapeDtypeStruct

D = 1024
DEPTH = 4
GW = 256
IN_W = 7 * GW
D_FF = 2816
SC_TAPS = 3
POOL_WINDOWS = (2, 4, 8, 16)
CF_TAPS = 31
SSM_G, SSM_H, SSM_P = 16, 16, 64
NST = SSM_G * SSM_P
ALPHA = (2 * DEPTH) ** 0.25
LN_EPS = 1e-5
ADAM_LR, ADAM_B1, ADAM_B2, ADAM_EPS, ADAM_WD, ADAM_STEP = 0.001, 0.9, 0.999, 1e-08, 0.01, 10

TR = 512
TK = 1024
HALO = 32
FF_BLK = 1408
VMEM_LIMIT = 56 * 1024 * 1024
N_DEV = 8
N_CHIPS = 4

MIXP_ROWS = 56
MIXP_CW, MIXP_CB, MIXP_LG, MIXP_LB, MIXP_PS = 8, 40, 41, 42, 48


def _cp(n_axes):
    return pltpu.CompilerParams(dimension_semantics=("arbitrary",) * n_axes, vmem_limit_bytes=VMEM_LIMIT)


def _sigmoid(x):
    return 0.5 * jnp.tanh(0.5 * x) + 0.5


_GELU_C = math.sqrt(2.0 / math.pi)


def _gelu(x):
    t = jnp.tanh(_GELU_C * (x + 0.044715 * (x * x * x)))
    return 0.5 * x * (1.0 + t), t


def _gelu_grad(x, t):
    return 0.5 * (1.0 + t) + 0.5 * x * (1.0 - t * t) * (_GELU_C * (1.0 + 3 * 0.044715 * (x * x)))


def _ln_stats(r):
    mu = jnp.mean(r, axis=-1, keepdims=True)
    xc = r - mu
    var = jnp.mean(xc * xc, axis=-1, keepdims=True)
    rstd = lax.rsqrt(var + LN_EPS)
    return xc * rstd, rstd


def _ln_bwd(dy, r, gamma):
    xhat, rstd = _ln_stats(r)
    dxh = dy * gamma
    m1 = jnp.mean(dxh, axis=-1, keepdims=True)
    m2 = jnp.mean(dxh * xhat, axis=-1, keepdims=True)
    return rstd * (dxh - m1 - xhat * m2), dy * xhat


def _rowsum(v):
    return jnp.sum(v, axis=0, keepdims=True)


def _dot(a, b):
    return jnp.dot(a, b, preferred_element_type=F32)


def _dot_nt(a, b):
    return lax.dot_general(a, b, (((1,), (1,)), ((), ())), preferred_element_type=F32)


def _dot_tn(a, b):
    return lax.dot_general(a, b, (((0,), (0,)), ((), ())), preferred_element_type=F32)


def _full(shape):
    return pl.BlockSpec(shape, lambda *_: (0,) * len(shape))


def mm_in_fwd(x, mod, w_t, b):
    s = x.shape[0]

    def body(x_ref, mod_ref, w_ref, b_ref, z_ref, h_ref):
        h = x_ref[...] * (1.0 + mod_ref[1:2, :]) + mod_ref[0:1, :]
        hb = h.astype(BF16)
        h_ref[...] = hb
        z_ref[...] = _dot_nt(hb, w_ref[...]) + b_ref[...]

    return pl.pallas_call(
        body, name="mm_in_fwd", grid=(s // TR,),
        in_specs=[pl.BlockSpec((TR, D), lambda i: (i, 0)), _full((8, D)), _full((IN_W, D)), _full((1, IN_W))],
        out_specs=[pl.BlockSpec((TR, IN_W), lambda i: (i, 0)), pl.BlockSpec((TR, D), lambda i: (i, 0))],
        out_shape=[SDS((s, IN_W), F32), SDS((s, D), BF16)], compiler_params=_cp(1),
    )(x, mod, w_t, b)


def _pool_consts(i, rows):
    lane = lax.broadcasted_iota(jnp.int32, (1, GW), 1) // (GW // 4)
    wl = jnp.where(lane == 0, 2.0, jnp.where(lane == 1, 4.0, jnp.where(lane == 2, 8.0, 16.0))).astype(F32)
    pos = (i * TR + 1 + lax.broadcasted_iota(jnp.int32, (rows, 1), 0)).astype(F32)
    return lane, jnp.minimum(pos, wl)


def _pick_window(lane, c2, c4, c8, c16):
    return jnp.where(lane == 0, c2, jnp.where(lane == 1, c4, jnp.where(lane == 2, c8, c16)))


CH = 64
PADR = 8


def _fill_shifted(src, col0, dst):
    rows = dst.shape[1]
    for r in range(8):
        dst[r] = src[pl.ds(r, rows), col0:col0 + GW]


def _tap(cp, start, r0, n=CH):
    return cp[start % 8, pl.ds(start - start % 8 + r0, n), :]


def mix_fwd(z, mixp, pblk):
    s = z.shape[0]
    nt = s // TR
    r = TR // HALO
    ext = HALO + TR

    def body(z_ref, zp_ref, p_ref, pblk_ref, y_ref, hc_ref, dms, buf, cpp, cpc):
        i = pl.program_id(0)
        zp = jnp.where(i > 0, zp_ref[...], 0.0)
        buf[ext:ext + PADR, :] = jnp.zeros((PADR, 768), F32)
        buf[0:HALO, 0:256] = zp[:, 512:768] * zp[:, 0:256]
        buf[HALO:ext, 0:256] = z_ref[:, 512:768] * z_ref[:, 0:256]
        buf[0:HALO, 256:512] = zp[:, 768:1024]
        buf[HALO:ext, 256:512] = z_ref[:, 768:1024]
        buf[0:HALO, 512:768] = zp[:, 1024:1280] * _sigmoid(zp[:, 1280:1536])
        buf[HALO:ext, 512:768] = z_ref[:, 1024:1280] * _sigmoid(z_ref[:, 1280:1536])
        _fill_shifted(buf, 256, cpp)
        _fill_shifted(buf, 512, cpc)
        lane, cnt = _pool_consts(i, TR)
        for r0 in range(0, TR, CH):
            rows = slice(r0, r0 + CH)
            cv = p_ref[0:1, :] * buf[pl.ds(HALO - 2 + r0, CH), 0:256]
            for j in range(1, SC_TAPS):
                cv = cv + p_ref[j:j + 1, :] * buf[pl.ds(HALO - 2 + j + r0, CH), 0:256]
            y_ref[rows, 0:256] = (z_ref[rows, 256:512] * cv).astype(BF16)
            acc = _tap(cpp, HALO, r0)
            caps = {}
            for k in range(1, 16):
                acc = acc + _tap(cpp, HALO - k, r0)
                if k + 1 in POOL_WINDOWS:
                    caps[k + 1] = acc
            dmean = _pick_window(lane, caps[2], caps[4], caps[8], caps[16]) / cnt[rows, :] - z_ref[rows, 768:1024]
            dms[rows, :] = dmean.astype(BF16)
            hc = p_ref[MIXP_CW:MIXP_CW + 1, :] * _tap(cpc, HALO - 30, r0)
            for j in range(1, CF_TAPS):
                hc = hc + p_ref[MIXP_CW + j:MIXP_CW + j + 1, :] * _tap(cpc, HALO - 30 + j, r0)
            hc = hc + p_ref[MIXP_CB:MIXP_CB + 1, :]
            hc_ref[rows, :] = hc
            xhat, _ = _ln_stats(hc)
            hn = xhat * p_ref[MIXP_LG:MIXP_LG + 1, :] + p_ref[MIXP_LB:MIXP_LB + 1, :]
            y_ref[rows, 512:768] = (hn * _sigmoid(hn)).astype(BF16)
        y_ref[:, 256:512] = (_dot(dms[...], pblk_ref[...]) * p_ref[MIXP_PS:MIXP_PS + 1, :]).astype(BF16)

    tile = pl.BlockSpec((TR, GW), lambda i: (i, 0))
    return pl.pallas_call(
        body, name="mix_fwd", grid=(nt,),
        in_specs=[pl.BlockSpec((TR, 1536), lambda i: (i, 0)),
                  pl.BlockSpec((HALO, 1536), lambda i: (jnp.maximum(i * r - 1, 0), 0)),
                  _full((MIXP_ROWS, GW)), _full((GW, GW))],
        out_specs=[pl.BlockSpec((TR, 768), lambda i: (i, 0)), tile, tile],
        out_shape=[SDS((s, D), BF16), SDS((s, GW), F32), SDS((s, GW), BF16)],
        scratch_shapes=[pltpu.VMEM((ext + PADR, 768), F32), pltpu.VMEM((8, ext, GW), F32), pltpu.VMEM((8, ext, GW), F32)],
        compiler_params=_cp(1),
    )(z, z, mixp, pblk)


SCAN_LANE_TILES = 8


def _scan_rows(st, base, nrows, tab_ref, carry_ref, reverse):
    ng = nrows // 8
    edge = slice(0, 1) if reverse else slice(7, 8)
    for j0 in range(0, NST // 128, SCAN_LANE_TILES):
        cols = [(slice(j * 128, (j + 1) * 128), slice(NST + j * 128, NST + (j + 1) * 128))
                for j in range(j0, j0 + SCAN_LANE_TILES)]

        def group(gi, carry, cols=cols):
            g = (ng - 1 - gi) if reverse else gi
            r0 = pl.multiple_of(base + g * 8, 8)
            out = []
            for (cr, ci), (c_r, c_i) in zip(cols, carry):
                xr = st[pl.ds(r0, 8), cr]
                xi = st[pl.ds(r0, 8), ci]
                for k, sft in enumerate((1, 2, 4)):
                    ar, ai = tab_ref[8 * k:8 * k + 8, cr], tab_ref[8 * k:8 * k + 8, ci]
                    amt = (8 - sft) if reverse else sft
                    rr = pltpu.roll(xr, amt, 0)
                    ri = pltpu.roll(xi, amt, 0)
                    xr, xi = xr + (ar * rr - ai * ri), xi + (ar * ri + ai * rr)
                pr, pi = tab_ref[24:32, cr], tab_ref[24:32, ci]
                xr, xi = xr + (pr * c_r - pi * c_i), xi + (pr * c_i + pi * c_r)
                st[pl.ds(r0, 8), cr] = xr
                st[pl.ds(r0, 8), ci] = xi
                out.append((jnp.broadcast_to(xr[edge, :], (8, 128)), jnp.broadcast_to(xi[edge, :], (8, 128))))
            return tuple(out)

        res = lax.fori_loop(0, ng, group, tuple((carry_ref[:, cr], carry_ref[:, ci]) for cr, ci in cols))
        for (cr, ci), (c_r, c_i) in zip(cols, res):
            carry_ref[:, cr] = c_r
            carry_ref[:, ci] = c_i


def ssm_fwd(z, ymix, bblk, cblk, tabf, ssmv, wglu):
    s = z.shape[0]
    nt = s // TR

    def body(u_ref, ymix_in, bblk_ref, cblk_ref, tab_ref, v_ref, wglu_ref, yd_ref, y_ref, xb_ref, st, carry):
        i = pl.program_id(0)

        @pl.when(i == 0)
        def _():
            carry[...] = jnp.zeros_like(carry)

        xb_ref[...] = carry[...]
        u = u_ref[...]
        st[...] = _dot(u.astype(BF16), bblk_ref[...])
        _scan_rows(st, 0, TR, tab_ref, carry, reverse=False)
        y = _dot(st[...].astype(BF16), cblk_ref[...]) + v_ref[0:1, :] * u
        y_ref[...] = y
        yg, _ = _gelu(y)
        q = _dot(yg.astype(BF16), wglu_ref[...]) + v_ref[1:2, :]
        yd_ref[...] = (yg * _sigmoid(q)).astype(BF16)

    return pl.pallas_call(
        body, name="ssm_fwd", grid=(nt,),
        in_specs=[pl.BlockSpec((TR, GW), lambda i: (i, 6)), pl.BlockSpec(memory_space=pl.ANY),
                  _full((GW, 2 * NST)), _full((2 * NST, GW)), _full((32, 2 * NST)), _full((8, GW)), _full((GW, GW))],
        out_specs=[pl.BlockSpec((TR, GW), lambda i: (i, 3)), pl.BlockSpec((TR, GW), lambda i: (i, 0)),
                   pl.BlockSpec((8, 2 * NST), lambda i: (i, 0))],
        out_shape=[SDS((s, D), BF16), SDS((s, GW), F32), SDS((nt * 8, 2 * NST), F32)],
        scratch_shapes=[pltpu.VMEM((TR, 2 * NST), F32), pltpu.VMEM((8, 2 * NST), F32)],
        input_output_aliases={1: 0}, compiler_params=_cp(1),
    )(z, ymix, bblk, cblk, tabf, ssmv, wglu)


def mm_res_ln(a, w, xres, mod, lnp, g_row, ln_row, h2_rows):
    s, k = a.shape

    def body(a_ref, w_ref, x_ref, mod_ref, ln_ref, f_ref, r_ref, xo_ref, *h_ref):
        f = _dot(a_ref[...], w_ref[...])
        f_ref[...] = f
        r = ALPHA * x_ref[...] + (1.0 + mod_ref[g_row:g_row + 1, :]) * f
        r_ref[...] = r
        xhat, _ = _ln_stats(r)
        xo = xhat * ln_ref[ln_row:ln_row + 1, :] + ln_ref[ln_row + 1:ln_row + 2, :]
        xo_ref[...] = xo
        if h2_rows is not None:
            sh, sc = h2_rows
            h_ref[0][...] = (xo * (1.0 + mod_ref[sc:sc + 1, :]) + mod_ref[sh:sh + 1, :]).astype(BF16)

    row = pl.BlockSpec((TR, D), lambda i: (i, 0))
    outs = [SDS((s, D), F32)] * 3 + ([SDS((s, D), BF16)] if h2_rows is not None else [])
    return pl.pallas_call(
        body, name="mm_res_ln_%d" % k, grid=(s // TR,),
        in_specs=[pl.BlockSpec((TR, k), lambda i: (i, 0)), _full((k, D)), row, _full((8, D)), _full((8, D))],
        out_specs=[row] * len(outs), out_shape=outs, compiler_params=_cp(1),
    )(a, w, xres, mod, lnp)


def mm_gate_up(h2, wg_t, wu_t, carry=()):
    s = h2.shape[0]
    nn, nt = D_FF // FF_BLK, s // TR
    na = len(carry)

    def body(*refs):
        h_ref, wg_ref, wu_ref = refs[:3]
        gt_ref, up_ref, a_ref = refs[3 + na:6 + na]
        if na:
            step = pl.program_id(0) * nt + pl.program_id(1)
            start, forward, finish = _gather_phases(refs[3:3 + na], refs[6 + na:6 + 2 * na], *refs[6 + 2 * na:])
            pl.when(step == 0)(start)
        h = h_ref[...]
        gt = _dot_nt(h, wg_ref[...])
        up = _dot_nt(h, wu_ref[...])
        gt_ref[...] = gt.astype(BF16)
        up_ref[...] = up.astype(BF16)
        a_ref[...] = (gt * _sigmoid(gt) * up).astype(BF16)
        if na:
            pl.when(step == nn * nt - 1 - max(1, nt // 4))(forward)
            pl.when(step == nn * nt - 1)(finish)

    wspec = pl.BlockSpec((FF_BLK, D), lambda n, i: (n, 0))
    ospec = pl.BlockSpec((TR, FF_BLK), lambda n, i: (i, n))
    sems = [pltpu.SemaphoreType.DMA((7 * na,)), pltpu.SemaphoreType.DMA((7 * na,)), pltpu.SemaphoreType.DMA((na,))]
    return pl.pallas_call(
        body, name="mm_gate_up_carry" if na else "mm_gate_up", grid=(nn, nt),
        in_specs=[pl.BlockSpec((TR, D), lambda n, i: (i, 0)), wspec, wspec] + [ANY] * na,
        out_specs=[ospec, ospec, ospec] + [ANY] * na,
        out_shape=[SDS((s, D_FF), BF16)] * 3 + [SDS((N_DEV * v.shape[0], v.shape[1]), v.dtype) for v in carry],
        scratch_shapes=sems if na else [], compiler_params=_cp(2),
    )(h2, wg_t, wu_t, *carry)


def loss_fwd(x, target):
    s = x.shape[0]

    def body(x_ref, t_ref, l_ref, dx_ref):
        i = pl.program_id(0)

        @pl.when(i == 0)
        def _():
            l_ref[...] = jnp.zeros_like(l_ref)

        e = x_ref[...] - t_ref[...]
        dx_ref[...] = e * (1.0 / D)
        part = jnp.sum(jnp.mean(e * e, axis=-1, keepdims=True), axis=0, keepdims=True)
        l_ref[...] = l_ref[...] + 0.5 * part

    row = pl.BlockSpec((TR, D), lambda i: (i, 0))
    return pl.pallas_call(
        body, name="loss_fwd", grid=(s // TR,), in_specs=[row, row], out_specs=[_full((8, 128)), row],
        out_shape=[SDS((8, 128), F32), SDS((s, D), F32)], compiler_params=_cp(1),
    )(x, target)


def bwd_ln_o(dx, r, y, w_o, mod, lnp):
    s = dx.shape[0]

    def body(dx_ref, r_ref, y_ref, w_ref, mod_ref, ln_ref, dr_ref, dy_ref, dm_ref, st_ref):
        i = pl.program_id(0)

        @pl.when(i == 0)
        def _():
            st_ref[...] = jnp.zeros_like(st_ref)

        dx = dx_ref[...]
        dr, dg = _ln_bwd(dx, r_ref[...], ln_ref[0:1, :])
        dr_ref[...] = dr
        dy = ((1.0 + mod_ref[2:3, :]) * dr).astype(BF16)
        dy_ref[...] = dy
        dm_ref[...] = _dot_nt(dy, w_ref[...])
        st_ref[0:1, :] += _rowsum(dr * y_ref[...])
        st_ref[1:2, :] += _rowsum(dg)
        st_ref[2:3, :] += _rowsum(dx)

    row = pl.BlockSpec((TR, D), lambda i: (i, 0))
    return pl.pallas_call(
        body, name="bwd_ln_o", grid=(s // TR,),
        in_specs=[row, row, row, _full((D, D)), _full((8, D)), _full((8, D))],
        out_specs=[row, row, row, _full((8, D))],
        out_shape=[SDS((s, D), F32), SDS((s, D), BF16), SDS((s, D), F32), SDS((8, D), F32)], compiler_params=_cp(1),
    )(dx, r, y, w_o, mod, lnp)


def bwd_ln_down(dx, r, f, gt, up, w_down, mod, lnp, carry=()):
    s = dx.shape[0]
    nn, nt = D_FF // FF_BLK, s // TR
    na = len(carry)

    def body(*refs):
        dx_ref, r_ref, f_ref, gt_ref, up_ref, w_ref, mod_ref, ln_ref = refs[:8]
        dr_ref, df_ref, dgt_ref, dup_ref, st_ref = refs[8 + na:13 + na]
        i = pl.program_id(0)
        n = pl.program_id(1)
        if na:
            start, finish = _scatter_phases(refs[8:8 + na], refs[13 + na:13 + 2 * na], *refs[13 + 2 * na:])
            pl.when((i == 0) & (n == 0))(start)

        @pl.when((i == 0) & (n == 0))
        def _():
            st_ref[...] = jnp.zeros_like(st_ref)

        @pl.when(n == 0)
        def _():
            dx = dx_ref[...]
            dr, dg = _ln_bwd(dx, r_ref[...], ln_ref[2:3, :])
            dr_ref[...] = dr
            df_ref[...] = ((1.0 + mod_ref[5:6, :]) * dr).astype(BF16)
            st_ref[0:1, :] += _rowsum(dr * f_ref[...])
            st_ref[1:2, :] += _rowsum(dg)
            st_ref[2:3, :] += _rowsum(dx)

        da = _dot_nt(df_ref[...], w_ref[...])
        gt = gt_ref[...].astype(F32)
        sg = _sigmoid(gt)
        dgt_ref[...] = (da * up_ref[...].astype(F32) * (sg * (1.0 + gt * (1.0 - sg)))).astype(BF16)
        dup_ref[...] = (da * (gt * sg)).astype(BF16)
        if na:
            pl.when((i == nt - 1) & (n == nn - 1))(finish)

    row = pl.BlockSpec((TR, D), lambda i, n: (i, 0))
    ff = pl.BlockSpec((TR, FF_BLK), lambda i, n: (i, n))
    par = pl.BlockSpec((8, D), lambda i, n: (0, 0))
    sems = [pltpu.SemaphoreType.DMA((3 * na,)), pltpu.SemaphoreType.DMA((3 * na,))]
    return pl.pallas_call(
        body, name="bwd_ln_down_carry" if na else "bwd_ln_down", grid=(nt, nn),
        in_specs=[row, row, row, ff, ff, pl.BlockSpec((FF_BLK, D), lambda i, n: (n, 0)), par, par] + [ANY] * na,
        out_specs=[row, row, ff, ff, par] + [ANY] * na,
        out_shape=[SDS((s, D), F32), SDS((s, D), BF16), SDS((s, D_FF), BF16), SDS((s, D_FF), BF16), SDS((8, D), F32)]
        + [SDS((3,) + p.shape[1:], p.dtype) for p in carry],
        scratch_shapes=sems if na else [], compiler_params=_cp(2),
    )(dx, r, f, gt, up, w_down, mod, lnp, *carry)


def bwd_dx_mod(parts, dr, xin, mod, sh_row, sc_row, name):
    s = dr.shape[0]
    npart = len(parts)

    def body(*refs):
        ins, (dx_ref, st_ref) = refs[:2 * npart + 3], refs[2 * npart + 3:]
        dr_ref, x_ref, mod_ref = ins[2 * npart:]
        i = pl.program_id(0)

        @pl.when(i == 0)
        def _():
            st_ref[...] = jnp.zeros_like(st_ref)

        dh = _dot(ins[0][...], ins[1][...])
        for k in range(1, npart):
            dh = dh + _dot(ins[2 * k][...], ins[2 * k + 1][...])
        dx_ref[...] = ALPHA * dr_ref[...] + dh * (1.0 + mod_ref[sc_row:sc_row + 1, :])
        st_ref[0:1, :] += _rowsum(dh)
        st_ref[1:2, :] += _rowsum(dh * x_ref[...])

    row = pl.BlockSpec((TR, D), lambda i: (i, 0))
    in_specs, args = [], []
    for g, w in parts:
        kk = g.shape[1]
        in_specs += [pl.BlockSpec((TR, kk), lambda i: (i, 0)), _full((kk, D))]
        args += [g, w]
    return pl.pallas_call(
        body, name=name, grid=(s // TR,), in_specs=in_specs + [row, row, _full((8, D))],
        out_specs=[row, _full((8, D))], out_shape=[SDS((s, D), F32), SDS((8, D), F32)], compiler_params=_cp(1),
    )(*args, dr, xin, mod)


def mm_tn(a, b, tm, name):
    s, m = a.shape
    n = b.shape[1]
    tk = min(TK, s)

    def body(a_ref, b_ref, o_ref):
        @pl.when(pl.program_id(1) == 0)
        def _():
            o_ref[...] = jnp.zeros_like(o_ref)

        o_ref[...] += _dot_tn(a_ref[...], b_ref[...])

    return pl.pallas_call(
        body, name=name, grid=(m // tm, s // tk),
        in_specs=[pl.BlockSpec((tk, tm), lambda j, k: (k, j)), pl.BlockSpec((tk, n), lambda j, k: (k, 0))],
        out_specs=pl.BlockSpec((tm, n), lambda j, k: (j, 0)), out_shape=SDS((m, n), F32), compiler_params=_cp(2),
    )(a, b)


def mix_bwd(z, dymix, hc, dmean, mixp, pblk):
    s = z.shape[0]
    nt = s // TR
    r = TR // HALO
    ext = TR + HALO

    def body(z_ref, zp_ref, zn_ref, dy_ref, dyn_ref, hc_ref, hcn_ref, dms, p_ref, pblk_ref,
             dz_ref, st_ref, dp_ref, db_ref, fbuf, bbuf, cbp, cfc, cbc, dos, dds, dpacc):
        i = pl.program_id(0)

        @pl.when(i == 0)
        def _():
            st_ref[...] = jnp.zeros_like(st_ref)
            dpacc[...] = jnp.zeros_like(dpacc)
            db_ref[...] = jnp.zeros_like(db_ref)

        zp = jnp.where(i > 0, zp_ref[...], 0.0)
        dyn = jnp.where(i < nt - 1, dyn_ref[...], 0.0)
        zeros = jnp.zeros((PADR, 768), F32)
        fbuf[ext:ext + PADR, :] = zeros
        bbuf[ext:ext + PADR, :] = zeros
        fbuf[0:HALO, 0:256] = zp[:, 512:768] * zp[:, 0:256]
        fbuf[HALO:ext, 0:256] = z_ref[:, 512:768] * z_ref[:, 0:256]
        fbuf[0:HALO, 512:768] = zp[:, 1024:1280] * _sigmoid(zp[:, 1280:1536])
        fbuf[HALO:ext, 512:768] = z_ref[:, 1024:1280] * _sigmoid(z_ref[:, 1280:1536])
        _fill_shifted(fbuf, 512, cfc)
        lane, cnt = _pool_consts(i, ext)
        ps = p_ref[MIXP_PS:MIXP_PS + 1, :]
        gam, bet = p_ref[MIXP_LG:MIXP_LG + 1, :], p_ref[MIXP_LB:MIXP_LB + 1, :]

        bbuf[TR:ext, 0:256] = dyn[:, 0:256] * zn_ref[...]
        bbuf[0:TR, 0:256] = dy_ref[:, 0:256] * z_ref[:, 256:512]
        dos[0:TR, :] = (dy_ref[:, 256:512] * ps).astype(BF16)
        dos[TR:ext, :] = (dyn[:, 256:512] * ps).astype(BF16)
        o = _dot(dms[...], pblk_ref[...])
        st_ref[MIXP_PS:MIXP_PS + 1, :] += _rowsum(dy_ref[:, 256:512] * o)
        dd = _dot_nt(dos[...], pblk_ref[...])
        dds[...] = dd[0:TR, :]
        bbuf[0:ext, 256:512] = dd / cnt
        dpacc[...] += _dot_tn(dms[...], dos[0:TR, :])
        _fill_shifted(bbuf, 256, cbp)

        s_g, s_b, s_c = (jnp.zeros((1, GW), F32),) * 3
        for r0 in range(0, ext, HALO):
            rows = slice(r0, r0 + HALO)
            if r0 < TR:
                hce, dye = hc_ref[rows, :], dy_ref[rows, 512:768]
            else:
                hce, dye = hcn_ref[...], dyn[:, 512:768]
            xhat, rstd = _ln_stats(hce)
            hn = xhat * gam + bet
            sh = _sigmoid(hn)
            dhn = dye * (sh * (1.0 + hn * (1.0 - sh)))
            dxh = dhn * gam
            m1 = jnp.mean(dxh, axis=-1, keepdims=True)
            m2 = jnp.mean(dxh * xhat, axis=-1, keepdims=True)
            dhc = rstd * (dxh - m1 - xhat * m2)
            bbuf[rows, 512:768] = dhc
            if r0 < TR:
                s_g, s_b, s_c = s_g + _rowsum(dhn * xhat), s_b + _rowsum(dhn), s_c + _rowsum(dhc)
        st_ref[MIXP_LG:MIXP_LG + 1, :] += s_g
        st_ref[MIXP_LB:MIXP_LB + 1, :] += s_b
        st_ref[MIXP_CB:MIXP_CB + 1, :] += s_c
        _fill_shifted(bbuf, 512, cbc)

        sums = [jnp.zeros((1, GW), F32)] * 6
        for r0 in range(0, TR, CH):
            rows = slice(r0, r0 + CH)
            cv = p_ref[0:1, :] * fbuf[pl.ds(HALO - 2 + r0, CH), 0:256]
            da = p_ref[0:1, :] * bbuf[pl.ds(2 + r0, CH), 0:256]
            for j in range(1, SC_TAPS):
                cv = cv + p_ref[j:j + 1, :] * fbuf[pl.ds(HALO - 2 + j + r0, CH), 0:256]
                da = da + p_ref[j:j + 1, :] * bbuf[pl.ds(2 - j + r0, CH), 0:256]
            dz_h = da * z_ref[rows, 512:768]
            dz_b = dy_ref[rows, 0:256] * cv
            dz_c = da * z_ref[rows, 0:256]
            acc = _tap(cbp, 0, r0)
            caps = {}
            for k in range(1, 16):
                acc = acc + _tap(cbp, k, r0)
                if k + 1 in POOL_WINDOWS:
                    caps[k + 1] = acc
            dz_p = _pick_window(lane, caps[2], caps[4], caps[8], caps[16]) - dds[rows, :]
            dhg = p_ref[MIXP_CW:MIXP_CW + 1, :] * _tap(cbc, 30, r0)
            for j in range(1, CF_TAPS):
                dhg = dhg + p_ref[MIXP_CW + j:MIXP_CW + j + 1, :] * _tap(cbc, 30 - j, r0)
            sg = _sigmoid(z_ref[rows, 1280:1536])
            dz_v = dhg * sg
            dz_g = dhg * z_ref[rows, 1024:1280] * (sg * (1.0 - sg))
            parts = (dz_h, dz_b, dz_c, dz_p, dz_v, dz_g)
            for k, v in enumerate(parts):
                dz_ref[rows, 256 * k:256 * k + 256] = v.astype(BF16)
            sums = [a + _rowsum(v) for a, v in zip(sums, parts)]
        for k in range(6):
            db_ref[0:1, 256 * k:256 * k + 256] += sums[k]

        for j in range(CF_TAPS):
            acc = bbuf[0:CH, 512:768] * _tap(cfc, HALO - 30 + j, 0)
            for r0 in range(CH, TR, CH):
                acc = acc + bbuf[r0:r0 + CH, 512:768] * _tap(cfc, HALO - 30 + j, r0)
            st_ref[MIXP_CW + j:MIXP_CW + j + 1, :] += _rowsum(acc)
        for j in range(SC_TAPS):
            st_ref[j:j + 1, :] += _rowsum(bbuf[0:TR, 0:256] * fbuf[pl.ds(HALO - 2 + j, TR), 0:256])

        @pl.when(i == nt - 1)
        def _():
            for k in range(4):
                dp_ref[64 * k:64 * k + 64, :] = dpacc[64 * k:64 * k + 64, 64 * k:64 * k + 64]

    nxt = lambda i: jnp.minimum((i + 1) * r, nt * r - 1)
    tile = pl.BlockSpec((TR, GW), lambda i: (i, 0))
    cp = pltpu.VMEM((8, ext, GW), F32)
    return pl.pallas_call(
        body, name="mix_bwd", grid=(nt,),
        in_specs=[pl.BlockSpec((TR, 1536), lambda i: (i, 0)),
                  pl.BlockSpec((HALO, 1536), lambda i: (jnp.maximum(i * r - 1, 0), 0)),
                  pl.BlockSpec((HALO, GW), lambda i: (nxt(i), 1)),
                  pl.BlockSpec((TR, 768), lambda i: (i, 0)), pl.BlockSpec((HALO, 768), lambda i: (nxt(i), 0)),
                  tile, pl.BlockSpec((HALO, GW), lambda i: (nxt(i), 0)), tile, _full((MIXP_ROWS, GW)), _full((GW, GW))],
        out_specs=[pl.BlockSpec((TR, 1536), lambda i: (i, 0)), _full((MIXP_ROWS, GW)), _full((GW, 64)), _full((8, 1536))],
        out_shape=[SDS((s, IN_W), BF16), SDS((MIXP_ROWS, GW), F32), SDS((GW, 64), F32), SDS((8, 1536), F32)],
        scratch_shapes=[pltpu.VMEM((ext + PADR, 768), F32), pltpu.VMEM((ext + PADR, 768), F32), cp, cp, cp,
                        pltpu.VMEM((ext, GW), BF16), pltpu.VMEM((TR, GW), F32), pltpu.VMEM((GW, GW), F32)],
        compiler_params=_cp(1),
    )(z, z, z, dymix, dymix, hc, hc, dmean, mixp, pblk)


def ssm_bwd(z, dz, dymix, y, xb, bblk, cblk, tabf, tabb, ssmv, wglu):
    s = z.shape[0]
    nt = s // TR

    def body(u_ref, dzin, dyd_ref, y_ref, xb_ref, bblk_ref, cblk_ref, tabf_ref, tabb_ref, v_ref, wglu_ref,
             dzs_ref, dbc_ref, dcc_ref, da_ref, dwg_ref, vst_ref, st, gs, carry, gcarry, dbacc, dcacc):
        i = pl.program_id(0)

        @pl.when(i == 0)
        def _():
            gcarry[...] = jnp.zeros_like(gcarry)
            for ref in (dbacc, dcacc, da_ref, dwg_ref, vst_ref):
                ref[...] = jnp.zeros_like(ref)

        u = u_ref[...]
        y = y_ref[...]
        yg, th = _gelu(y)
        ygb = yg.astype(BF16)
        q = _dot(ygb, wglu_ref[...]) + v_ref[1:2, :]
        sq = _sigmoid(q)
        dout = dyd_ref[...]
        dq = dout * yg * (sq * (1.0 - sq))
        dqb = dq.astype(BF16)
        dyg = dout * sq + _dot_nt(dqb, wglu_ref[...])
        dy = dyg * _gelu_grad(y, th)
        dyb = dy.astype(BF16)
        dwg_ref[...] += _dot_tn(ygb, dqb)
        vst_ref[0:1, :] += _rowsum(dy * u)
        vst_ref[1:2, :] += _rowsum(dq)
        ub = u.astype(BF16)
        carry[...] = xb_ref[...]
        st[0:8, :] = xb_ref[...]
        st[8:8 + TR, :] = _dot(ub, bblk_ref[...])
        _scan_rows(st, 8, TR, tabf_ref, carry, reverse=False)
        gs[...] = _dot_nt(dyb, cblk_ref[...])
        _scan_rows(gs, 0, TR, tabb_ref, gcarry, reverse=True)
        xs = st[pl.ds(8, TR), :]
        dcacc[...] += _dot_tn(dyb, xs.astype(BF16))
        g = gs[...]
        gb = g.astype(BF16)
        dbacc[...] += _dot_tn(ub, gb)
        gr, gi = g[:, 0:NST], g[:, NST:]
        xp = st[pl.ds(7, TR), :]
        xr, xi = xp[:, 0:NST], xp[:, NST:]
        da_ref[0:1, 0:NST] += _rowsum(gr * xr + gi * xi)
        da_ref[0:1, NST:] += _rowsum(gi * xr - gr * xi)
        du = _dot_nt(gb, bblk_ref[...]) + v_ref[0:1, :] * dy
        dzs_ref[...] = du.astype(BF16)
        vst_ref[2:3, :] += _rowsum(du)

        @pl.when(i == nt - 1)
        def _():
            for g_ in range(SSM_G):
                rows = slice(g_ * SSM_H, (g_ + 1) * SSM_H)
                for acc, out in ((dbacc, dbc_ref), (dcacc, dcc_ref)):
                    out[rows, 0:SSM_P] = acc[rows, g_ * SSM_P:(g_ + 1) * SSM_P]
                    out[rows, SSM_P:2 * SSM_P] = acc[rows, NST + g_ * SSM_P:NST + (g_ + 1) * SSM_P]

    rev = lambda i: nt - 1 - i
    return pl.pallas_call(
        body, name="ssm_bwd", grid=(nt,),
        in_specs=[pl.BlockSpec((TR, GW), lambda i: (rev(i), 6)), pl.BlockSpec(memory_space=pl.ANY),
                  pl.BlockSpec((TR, GW), lambda i: (rev(i), 3)), pl.BlockSpec((TR, GW), lambda i: (rev(i), 0)),
                  pl.BlockSpec((8, 2 * NST), lambda i: (rev(i), 0)),
                  _full((GW, 2 * NST)), _full((2 * NST, GW)), _full((32, 2 * NST)), _full((32, 2 * NST)),
                  _full((8, GW)), _full((GW, GW))],
        out_specs=[pl.BlockSpec((TR, GW), lambda i: (rev(i), 6)), _full((GW, 2 * SSM_P)), _full((GW, 2 * SSM_P)),
                   _full((8, 2 * NST)), _full((GW, GW)), _full((8, GW))],
        out_shape=[SDS((s, IN_W), BF16), SDS((GW, 2 * SSM_P), F32), SDS((GW, 2 * SSM_P), F32), SDS((8, 2 * NST), F32),
                   SDS((GW, GW), F32), SDS((8, GW), F32)],
        scratch_shapes=[pltpu.VMEM((8 + TR, 2 * NST), F32), pltpu.VMEM((TR, 2 * NST), F32),
                        pltpu.VMEM((8, 2 * NST), F32), pltpu.VMEM((8, 2 * NST), F32),
                        pltpu.VMEM((GW, 2 * NST), F32), pltpu.VMEM((GW, 2 * NST), F32)],
        input_output_aliases={1: 0}, compiler_params=_cp(1),
    )(z, dz, dymix, y, xb, bblk, cblk, tabf, tabb, ssmv, wglu)


def _ssm_prep(lam_re, lam_im, log_dt, b_re, b_im):
    dt = jnp.exp(log_dt)[..., None]
    mag = jnp.exp(lam_re * dt)
    ar, ai = mag * jnp.cos(lam_im * dt), mag * jnp.sin(lam_im * dt)
    den = lam_re * lam_re + lam_im * lam_im
    qr = ((ar - 1.0) * lam_re + ai * lam_im) / den
    qi = (ai * lam_re - (ar - 1.0) * lam_im) / den
    bbr = qr[..., None] * b_re - qi[..., None] * b_im
    bbi = qr[..., None] * b_im + qi[..., None] * b_re
    return ar, ai, bbr, bbi


def _ssm_tables(lam_re, lam_im, log_dt):
    nl = lam_re.shape[0]
    dt = jnp.exp(log_dt)[:, None, :, None]
    k = jnp.arange(1, 9, dtype=F32)[None, :, None, None]
    mag = jnp.exp(k * (lam_re[:, None] * dt))
    ang = k * (lam_im[:, None] * dt)
    pr = (mag * jnp.cos(ang)).reshape(nl, 8, NST)
    pi = (mag * jnp.sin(ang)).reshape(nl, 8, NST)
    row = jnp.arange(8)[None, :, None]

    def table(sign, reverse):
        parts_r, parts_i = [], []
        for sft in (1, 2, 4):
            keep = (row < 8 - sft) if reverse else (row >= sft)
            parts_r.append(jnp.where(keep, pr[:, sft - 1:sft], 0.0))
            parts_i.append(jnp.where(keep, sign * pi[:, sft - 1:sft], 0.0))
        parts_r.append(pr[:, ::-1] if reverse else pr)
        parts_i.append(sign * (pi[:, ::-1] if reverse else pi))
        return jnp.concatenate([jnp.concatenate(parts_r, axis=1), jnp.concatenate(parts_i, axis=1)], axis=2)

    return table(1.0, False), table(-1.0, True)


def _blockdiag(m):
    nl, g, a, b = m.shape
    return jnp.einsum('lgab,gk->lgakb', m, jnp.eye(g, dtype=m.dtype)).reshape(nl, g * a, g * b)


def _rows_at(blocks, total):
    out, at = [], 0
    nl, _, c = blocks[0][1].shape
    for r0, b in blocks:
        if r0 > at:
            out.append(jnp.zeros((nl, r0 - at, c), F32))
        out.append(b)
        at = r0 + b.shape[1]
    if total > at:
        out.append(jnp.zeros((nl, total - at, c), F32))
    return jnp.concatenate(out, axis=1)


def prep_params(p):
    ar, ai, bbr, bbi = _ssm_prep(p['ssm_lam_re'], p['ssm_lam_im'], p['ssm_log_dt'], p['ssm_b_re'], p['ssm_b_im'])
    bblk = jnp.concatenate([_blockdiag(jnp.swapaxes(bbr, 2, 3)), _blockdiag(jnp.swapaxes(bbi, 2, 3))], axis=2)
    cblk = jnp.concatenate([_blockdiag(jnp.swapaxes(p['ssm_c_re'], 2, 3)),
                            -_blockdiag(jnp.swapaxes(p['ssm_c_im'], 2, 3))], axis=1)
    tabf, tabb = _ssm_tables(p['ssm_lam_re'], p['ssm_lam_im'], p['ssm_log_dt'])
    mixp = _rows_at([(0, p['sc_w']), (MIXP_CW, p['cf_dw_w']), (MIXP_CB, p['cf_dw_b'][:, None]),
                     (MIXP_LG, p['cf_ln_g'][:, None]), (MIXP_LB, p['cf_ln_b'][:, None]),
                     (MIXP_PS, p['pool_scale'][:, None])], MIXP_ROWS)
    return dict(
        mixp=mixp, pblk=_blockdiag(p['pool_w']).astype(BF16), bblk=bblk.astype(BF16), cblk=cblk.astype(BF16),
        tabf=tabf, tabb=tabb, ssmv=_rows_at([(0, p['ssm_d'][:, None]), (1, p['ssm_b_glu'][:, None])], 8),
        lnp=_rows_at([(0, p['ln1_g'][:, None]), (1, p['ln1_b'][:, None]), (2, p['ln2_g'][:, None]),
                      (3, p['ln2_b'][:, None])], 8),
        b_in=p['b_in'][:, None],
    )


def layer_fwd(x, mod, w, q, next_halves=()):
    z, h1 = mm_in_fwd(x, mod, w['w_in'], q['b_in'])
    ymix, hc, dmean = mix_fwd(z, q['mixp'], q['pblk'])
    ymix, ys, xb = ssm_fwd(z, ymix, q['bblk'], q['cblk'], q['tabf'], q['ssmv'], w['ssm_w_glu'])
    y, r1, x1, h2 = mm_res_ln(ymix, w['w_o'], x, mod, q['lnp'], 2, 0, (3, 4))
    gt, up, act, *w_next = mm_gate_up(h2, w['w_gate'], w['w_up'], next_halves)
    f, r2, x2 = mm_res_ln(act, w['w_down'], x1, mod, q['lnp'], 5, 2, None)
    saved = dict(x=x, z=z, h1=h1, ymix=ymix, hc=hc, dmean=dmean, ys=ys, xb=xb, y=y, r1=r1, x1=x1, h2=h2, gt=gt, up=up,
                 act=act, f=f, r2=r2)
    return x2, saved, w_next


def layer_bwd(dx2, mod, w, q, sv, pending=()):
    dr2, df, dgt, dup, st2, *recv = bwd_ln_down(dx2, sv['r2'], sv['f'], sv['gt'], sv['up'], w['w_down'], mod, q['lnp'],
                                                pending)
    dx1, stm2 = bwd_dx_mod([(dgt, w['w_gate']), (dup, w['w_up'])], dr2, sv['x1'], mod, 3, 4, "bwd_dx_ff")
    g_down = mm_tn(sv['act'], df, FF_BLK, "dw_down")
    g_gate = mm_tn(dgt, sv['h2'], FF_BLK, "dw_gate")
    g_up = mm_tn(dup, sv['h2'], FF_BLK, "dw_up")
    dr1, dy, dymix, st1 = bwd_ln_o(dx1, sv['r1'], sv['y'], w['w_o'], mod, q['lnp'])
    g_o = mm_tn(sv['ymix'], dy, D // 2, "dw_o")
    dz, mst, dpool, dbin = mix_bwd(sv['z'], dymix, sv['hc'], sv['dmean'], q['mixp'], q['pblk'])
    dz, dbc, dcc, da, g_glu, vst = ssm_bwd(sv['z'], dz, dymix, sv['ys'], sv['xb'], q['bblk'], q['cblk'], q['tabf'],
                                           q['tabb'], q['ssmv'], w['ssm_w_glu'])
    dx, stm1 = bwd_dx_mod([(dz, w['w_in'])], dr1, sv['x'], mod, 0, 1, "bwd_dx_in")
    g_in = mm_tn(dz, sv['h1'], IN_W // 2, "dw_in")
    big = [g_in, g_o, g_gate, g_up, g_down, g_glu]
    stats = dict(s1024=jnp.concatenate([stm1, st1, stm2, st2], axis=0), mst=mst, vst=vst, dpool=dpool,
                 dbc=dbc, dcc=dcc, da=da, dbin=dbin)
    return dx, big, stats, recv


MESH = pl.DeviceIdType.MESH
ANY = pl.BlockSpec(memory_space=pl.ANY)
VMEM_SPEC = pl.BlockSpec(memory_space=pltpu.VMEM)


def _place():
    return lax.axis_index("x"), lax.axis_index("y"), lax.axis_index("c")


def _gather_phases(x_refs, out_refs, send_sems, recv_sems, local_sems):
    na = len(x_refs)

    def parts():
        x, y, c = _place()
        chips = [(1 - x, y), (x, 1 - y), (1 - x, 1 - y)]

        def rows(a, px, py, pc):
            m_per = x_refs[a].shape[0]
            return out_refs[a].at[pl.ds((4 * px + 2 * py + pc) * m_per, m_per), :]

        def copy(a, k, block, to, src=None):
            return pltpu.make_async_remote_copy(
                src_ref=rows(a, *block) if src is None else src, dst_ref=rows(a, *block),
                send_sem=send_sems.at[7 * a + k], recv_sem=recv_sems.at[7 * a + k], device_id=to, device_id_type=MESH)

        def mine():
            return [pltpu.make_async_copy(x_refs[a], rows(a, x, y, c), local_sems.at[a]) for a in range(na)]

        def first():
            out = []
            for a in range(na):
                out.append(copy(a, 0, (x, y, c), (x, y, 1 - c), src=x_refs[a]))
                out += [copy(a, 1 + j, (x, y, c), (*chip, c), src=x_refs[a]) for j, chip in enumerate(chips)]
            return out

        def passed(j, a):
            return copy(a, 4 + j, (*chips[j], c), (x, y, 1 - c))

        return (x, y, c), chips, copy, mine, first, passed

    def start():
        _, _, _, mine, first, _ = parts()
        for cp in mine() + first():
            cp.start()

    def forward():
        me, chips, copy, _, _, passed = parts()
        for j, chip in enumerate(chips):
            for a in range(na):
                copy(a, 1 + j, (*chip, me[2]), me).wait_recv()
                passed(j, a).start()

    def finish():
        me, chips, copy, mine, first, passed = parts()
        for a in range(na):
            copy(a, 0, (me[0], me[1], 1 - me[2]), me).wait_recv()
            for j, chip in enumerate(chips):
                copy(a, 4 + j, (*chip, 1 - me[2]), me).wait_recv()
        for cp in first() + [passed(j, a) for j in range(3) for a in range(na)]:
            cp.wait_send()
        for cp in mine():
            cp.wait()

    return start, forward, finish


def _scatter_phases(ps_refs, out_refs, send_sems, recv_sems):
    na = len(ps_refs)

    def copies():
        x, y, c = _place()
        chips = [(1 - x, y), (x, 1 - y), (1 - x, 1 - y)]
        return [pltpu.make_async_remote_copy(
            src_ref=ps_refs[a].at[2 * kx + ky], dst_ref=out_refs[a].at[j], send_sem=send_sems.at[3 * a + j],
            recv_sem=recv_sems.at[3 * a + j], device_id=(kx, ky, c), device_id_type=MESH)
            for a in range(na) for j, (kx, ky) in enumerate(chips)]

    def start():
        for cp in copies():
            cp.start()

    def finish():
        for cp in copies():
            cp.wait()

    return start, finish


def allgather8(vs, name):
    na = len(vs)

    def body(*refs):
        start, forward, finish = _gather_phases(refs[:na], refs[na:2 * na], *refs[2 * na:])
        start()
        forward()
        finish()

    return pl.pallas_call(
        body, name=name, out_shape=[SDS((N_DEV * v.shape[0], v.shape[1]), v.dtype) for v in vs],
        in_specs=[VMEM_SPEC] * na, out_specs=[VMEM_SPEC] * na,
        scratch_shapes=[pltpu.SemaphoreType.DMA((7 * na,)), pltpu.SemaphoreType.DMA((7 * na,)),
                        pltpu.SemaphoreType.DMA((na,))],
        compiler_params=pltpu.CompilerParams(vmem_limit_bytes=VMEM_LIMIT),
    )(*vs)


def pair_swap(gs):
    na = len(gs)

    def body(*refs):
        g_refs, out_refs, (send_sems, recv_sems) = refs[:na], refs[na:2 * na], refs[2 * na:]
        x, y, c = _place()
        cps = [pltpu.make_async_remote_copy(
            src_ref=g_refs[a].at[j, 1 - c], dst_ref=out_refs[a].at[j], send_sem=send_sems.at[4 * a + j],
            recv_sem=recv_sems.at[4 * a + j], device_id=(x, y, 1 - c), device_id_type=MESH)
            for a in range(na) for j in range(N_CHIPS)]
        for cp in cps:
            cp.start()
        for cp in cps:
            cp.wait()

    return pl.pallas_call(
        body, name="pair_swap", out_shape=[SDS((g.shape[0],) + g.shape[2:], g.dtype) for g in gs],
        in_specs=[ANY] * na, out_specs=[ANY] * na,
        scratch_shapes=[pltpu.SemaphoreType.DMA((4 * na,)), pltpu.SemaphoreType.DMA((4 * na,))],
    )(*gs)


def chip_scatter(pss):
    na = len(pss)

    def body(*refs):
        start, finish = _scatter_phases(refs[:na], refs[na:2 * na], *refs[2 * na:])
        start()
        finish()

    return pl.pallas_call(
        body, name="chip_scatter", out_shape=[SDS((3,) + p.shape[1:], p.dtype) for p in pss],
        in_specs=[ANY] * na, out_specs=[ANY] * na,
        scratch_shapes=[pltpu.SemaphoreType.DMA((3 * na,)), pltpu.SemaphoreType.DMA((3 * na,))],
    )(*pss)


def pair_gather(ts):
    na = len(ts)

    def body(*refs):
        t_refs, out_refs, (send_sems, recv_sems) = refs[:na], refs[na:2 * na], refs[2 * na:]
        x, y, c = _place()
        cps = [pltpu.make_async_remote_copy(
            src_ref=t_refs[a].at[c], dst_ref=out_refs[a].at[c], send_sem=send_sems.at[a], recv_sem=recv_sems.at[a],
            device_id=(x, y, 1 - c), device_id_type=MESH) for a in range(na)]
        for cp in cps:
            cp.start()
        for cp in cps:
            cp.wait()

    return pl.pallas_call(
        body, name="pair_gather", out_shape=[SDS(t.shape, t.dtype) for t in ts],
        in_specs=[ANY] * na, out_specs=[ANY] * na, input_output_aliases={a: a for a in range(na)},
        scratch_shapes=[pltpu.SemaphoreType.DMA((na,)), pltpu.SemaphoreType.DMA((na,))],
    )(*ts)


def _scalar(v):
    return jnp.reshape(v, (1,)).astype(jnp.int32)


def pair_sum(gs, recvs):
    na = len(gs)
    c = lax.axis_index("c")

    def body(c_ref, *refs):
        for a in range(na):
            refs[2 * na + a][...] = (refs[a][...] + refs[na + a][...]).astype(BF16)

    in_specs = [pl.BlockSpec((None, None) + g.shape[2:], lambda j, c_ref: (j, c_ref[0], 0, 0)) for g in gs]
    in_specs += [pl.BlockSpec((None,) + r.shape[1:], lambda j, c_ref: (j, 0, 0)) for r in recvs]
    return pl.pallas_call(
        body, name="pair_sum",
        grid_spec=pltpu.PrefetchScalarGridSpec(
            num_scalar_prefetch=1, grid=(N_CHIPS,), in_specs=in_specs,
            out_specs=[pl.BlockSpec((None,) + r.shape[1:], lambda j, c_ref: (j, 0, 0)) for r in recvs]),
        out_shape=[SDS(r.shape, BF16) for r in recvs], compiler_params=_cp(1),
    )(_scalar(c), *gs, *recvs)


def sum_chips(pss, recvs):
    na = len(pss)
    xi, yi, ci = _place()

    def body(s_ref, *refs):
        for a in range(na):
            r = refs[na + a]
            acc = refs[a][...].astype(F32) + r[0].astype(F32)
            acc = acc + r[1].astype(F32)
            refs[2 * na + a][...] = acc + r[2].astype(F32)

    in_specs = [pl.BlockSpec((None,) + p.shape[1:], lambda i, s_ref: (s_ref[0], 0, 0)) for p in pss]
    in_specs += [pl.BlockSpec(r.shape, lambda i, s_ref: (0, 0, 0)) for r in recvs]
    return pl.pallas_call(
        body, name="sum_chips",
        grid_spec=pltpu.PrefetchScalarGridSpec(
            num_scalar_prefetch=1, grid=(1,), in_specs=in_specs,
            out_specs=[pl.BlockSpec((None,) + p.shape[1:], lambda i, s_ref: (s_ref[1], 0, 0)) for p in pss]),
        out_shape=[SDS((2,) + p.shape[1:], F32) for p in pss], compiler_params=_cp(1),
    )(jnp.stack([2 * xi + yi, ci]).astype(jnp.int32), *pss, *recvs)


def sum8(vs, rows):
    na = len(vs)

    def body(*refs):
        for a in range(na):
            m = rows[a]
            acc = refs[a][0:m, :]
            for d in range(1, N_DEV):
                acc = acc + refs[a][d * m:(d + 1) * m, :]
            refs[na + a][...] = acc

    return pl.pallas_call(
        body, name="sum8", out_shape=[SDS((rows[a], vs[a].shape[1]), F32) for a in range(na)],
        in_specs=[VMEM_SPEC] * na, out_specs=[VMEM_SPEC] * na,
        compiler_params=pltpu.CompilerParams(vmem_limit_bytes=VMEM_LIMIT),
    )(*vs)


def ada_fwd(c_all, w_ada):
    nl, _, n = w_ada.shape
    bn = 512

    def body(c_ref, w_ref, o_ref):
        cv = c_ref[...]
        cond = (cv * _sigmoid(cv)).astype(BF16)
        o_ref[...] = _dot(cond, w_ref[...].astype(BF16))

    return pl.pallas_call(
        body, name="ada_fwd", grid=(nl, n // bn),
        in_specs=[pl.BlockSpec((8, D), lambda l, j: (0, 0)), pl.BlockSpec((None, D, bn), lambda l, j: (l, 0, j))],
        out_specs=pl.BlockSpec((None, 8, bn), lambda l, j: (l, 0, j)), out_shape=SDS((nl, 8, n), F32),
        compiler_params=_cp(2),
    )(c_all, w_ada)


def ada_grad(c_all, dm):
    nl, _, n = dm.shape
    bn = 512

    def body(c_ref, d_ref, o_ref):
        cv = c_ref[...]
        cond = (cv * _sigmoid(cv)).astype(BF16)
        o_ref[...] = _dot_tn(cond, d_ref[...].astype(BF16))

    return pl.pallas_call(
        body, name="ada_grad", grid=(nl, n // bn),
        in_specs=[pl.BlockSpec((8, D), lambda l, j: (0, 0)), pl.BlockSpec((None, 8, bn), lambda l, j: (l, 0, j))],
        out_specs=pl.BlockSpec((None, D, bn), lambda l, j: (l, 0, j)), out_shape=SDS((nl, D, n), F32),
        compiler_params=_cp(2),
    )(c_all, dm)


def _adam_math(w, g, m, v):
    m = ADAM_B1 * m + (1.0 - ADAM_B1) * g
    v = ADAM_B2 * v + (1.0 - ADAM_B2) * (g * g)
    m_hat = m / (1.0 - ADAM_B1 ** ADAM_STEP)
    v_hat = v / (1.0 - ADAM_B2 ** ADAM_STEP)
    return -ADAM_LR * (m_hat / (jnp.sqrt(v_hat) + ADAM_EPS) + ADAM_WD * w), m, v


def adamw(w, g, m, v, name):
    rr, cc = w.shape
    tr = 256

    def body(w_ref, g_ref, m_ref, v_ref, d_ref, mo_ref, vo_ref):
        d_ref[...], mo_ref[...], vo_ref[...] = _adam_math(w_ref[...], g_ref[...], m_ref[...], v_ref[...])

    spec = pl.BlockSpec((tr, cc), lambda i: (i, 0))
    return pl.pallas_call(
        body, name=name, grid=(rr // tr,), in_specs=[spec] * 4, out_specs=[spec] * 3,
        out_shape=[SDS((rr, cc), F32)] * 3, compiler_params=_cp(1),
    )(w, g, m, v)


def adamw_many(ws, gs, ms, vs):
    na = len(ws)

    def body(*refs):
        for a in range(na):
            d, m, v = _adam_math(refs[a][...], refs[na + a][...], refs[2 * na + a][...], refs[3 * na + a][...])
            refs[4 * na + a][...] = d
            refs[5 * na + a][...] = m
            refs[6 * na + a][...] = v

    outs = [SDS(w.shape, F32) for w in ws] * 3
    res = pl.pallas_call(
        body, name="adamw_small", out_shape=outs, in_specs=[VMEM_SPEC] * (4 * na), out_specs=[VMEM_SPEC] * (3 * na),
        compiler_params=pltpu.CompilerParams(vmem_limit_bytes=VMEM_LIMIT),
    )(*ws, *gs, *ms, *vs)
    return res[:na], res[na:2 * na], res[2 * na:]


BIG = ('w_in', 'w_o', 'w_gate', 'w_up', 'w_down', 'ssm_w_glu')
BIG_T = ('w_in', 'w_gate', 'w_up')
WEIGHTS = ['w_ada', 'b_ada', 'w_in', 'b_in', 'sc_w', 'pool_w', 'pool_scale', 'cf_dw_w', 'cf_dw_b', 'cf_ln_g', 'cf_ln_b',
           'ssm_lam_re', 'ssm_lam_im', 'ssm_log_dt', 'ssm_b_re', 'ssm_b_im', 'ssm_c_re', 'ssm_c_im', 'ssm_d', 'ssm_w_glu',
           'ssm_b_glu', 'w_o', 'ln1_g', 'ln1_b', 'w_gate', 'w_up', 'w_down', 'ln2_g', 'ln2_b']
STAT_KINDS = ('s1024', 'mst', 'vst', 'dpool', 'dbc', 'dcc', 'da', 'dbin')


def _chip_blocks(g, width):
    m = g.shape[0] // N_DEV
    return g.reshape(N_CHIPS, 2, m, g.shape[1])[:, 0, :, :width]


def kernel(x, c, w_ada, b_ada, w_in, b_in, sc_w, pool_w, pool_scale, cf_dw_w, cf_dw_b, cf_ln_g, cf_ln_b, ssm_lam_re, ssm_lam_im, ssm_log_dt, ssm_b_re, ssm_b_im, ssm_c_re, ssm_c_im, ssm_d, ssm_w_glu, ssm_b_glu, w_o, ln1_g, ln1_b, w_gate, w_up, w_down, ln2_g, ln2_b, loss_target, m_w_ada, m_b_ada, m_w_in, m_b_in, m_sc_w, m_pool_w, m_pool_scale, m_cf_dw_w, m_cf_dw_b, m_cf_ln_g, m_cf_ln_b, m_ssm_lam_re, m_ssm_lam_im, m_ssm_log_dt, m_ssm_b_re, m_ssm_b_im, m_ssm_c_re, m_ssm_c_im, m_ssm_d, m_ssm_w_glu, m_ssm_b_glu, m_w_o, m_ln1_g, m_ln1_b, m_w_gate, m_w_up, m_w_down, m_ln2_g, m_ln2_b, v_w_ada, v_b_ada, v_w_in, v_b_in, v_sc_w, v_pool_w, v_pool_scale, v_cf_dw_w, v_cf_dw_b, v_cf_ln_g, v_cf_ln_b, v_ssm_lam_re, v_ssm_lam_im, v_ssm_log_dt, v_ssm_b_re, v_ssm_b_im, v_ssm_c_re, v_ssm_c_im, v_ssm_d, v_ssm_w_glu, v_ssm_b_glu, v_w_o, v_ln1_g, v_ln1_b, v_w_gate, v_w_up, v_w_down, v_ln2_g, v_ln2_b):
    a = dict(locals())
    W = {n: a[n] for n in WEIGHTS}
    M = {n: a['m_' + n] for n in WEIGHTS}
    V = {n: a['v_' + n] for n in WEIGHTS}
    xi, yi, ci = _place()
    me = 4 * xi + 2 * yi + ci
    chip = 2 * xi + yi
    nl = DEPTH

    g_c, g_sc, g_cf = allgather8(
        [jnp.pad(c, ((0, 7), (0, 0))), jnp.pad(sc_w.reshape(nl * SC_TAPS, 64), ((0, 4), (0, 0))),
         jnp.pad(cf_dw_w.reshape(nl * CF_TAPS, 64), ((0, 4), (0, 0)))], "gather_pre")
    c_all = g_c.reshape(N_DEV, 8, D)[:, 0]
    sc_full = jnp.moveaxis(_chip_blocks(g_sc, 64)[:, :nl * SC_TAPS].reshape(N_CHIPS, nl, SC_TAPS, 64), 0, 2)
    sc_full = sc_full.reshape(nl, SC_TAPS, GW)
    cf_full = jnp.moveaxis(_chip_blocks(g_cf, 64)[:, :nl * CF_TAPS].reshape(N_CHIPS, nl, CF_TAPS, 64), 0, 2)
    cf_full = cf_full.reshape(nl, CF_TAPS, GW)

    mod_part = ada_fwd(c_all, w_ada)
    (g_mod,) = allgather8([mod_part.reshape(nl * 8, -1)], "gather_mod")
    mod_all = jnp.moveaxis(_chip_blocks(g_mod, 6 * D // N_CHIPS).reshape(N_CHIPS, nl, 8, -1), 0, 2)
    mod_all = mod_all.reshape(nl, 8, 6 * D) + b_ada[:, None, :]
    mod_mine = lax.dynamic_index_in_dim(mod_all, me, axis=1, keepdims=False).reshape(nl, 6, D)
    mods = jnp.pad(mod_mine, ((0, 0), (0, 2), (0, 0)))

    halves = {}
    for n in BIG:
        wt = jnp.swapaxes(W[n], 1, 2) if n in BIG_T else W[n]
        hr = wt.shape[1] // 2
        halves[n] = lax.dynamic_slice_in_dim(wt, ci * hr, hr, axis=1).astype(BF16)
    half_l = [[halves[n][l] for n in BIG] for l in range(nl)]
    wfull = [dict(zip(BIG, allgather8(half_l[0], "gather_w")))]

    small_names = [n for n in WEIGHTS if n not in BIG and n not in ('w_ada', 'b_ada')]
    pfull = {n: W[n] for n in small_names}
    pfull['sc_w'], pfull['cf_dw_w'] = sc_full, cf_full
    Q = prep_params(pfull)
    h = x[0]
    saved = []
    for l in range(nl):
        h, sv, w_next = layer_fwd(h, mods[l], wfull[l], {k: v[l] for k, v in Q.items()},
                                  half_l[l + 1] if l + 1 < nl else ())
        saved.append(sv)
        if w_next:
            wfull.append(dict(zip(BIG, w_next)))
    l8, dh = loss_fwd(h, loss_target[0])
    loss = lax.psum(l8[0, 0], ("x", "y", "c"))

    stats = [None] * nl
    gbig = {n: [None] * nl for n in BIG}

    def finish_reduce(l, pss, recv):
        for n, t in zip(BIG, pair_gather(sum_chips(pss, recv))):
            gbig[n][l] = t.reshape(2 * t.shape[1], t.shape[2])

    pending = ()
    for l in reversed(range(nl)):
        dh, big, stats[l], recv = layer_bwd(dh, mods[l], wfull[l], {k: v[l] for k, v in Q.items()}, saved[l], pending)
        if pending:
            finish_reduce(l + 1, pending, recv)
        views = [g.reshape(N_CHIPS, 2, g.shape[0] // (2 * N_CHIPS), g.shape[1]) for g in big]
        pending = pair_sum(views, pair_swap(views))
    finish_reduce(0, pending, chip_scatter(pending))
    grad_x = dh[None]

    mine = [jnp.concatenate([stats[l][k] for l in range(nl)], axis=0) for k in STAT_KINDS]
    rows = [v.shape[0] for v in mine]
    gathered = allgather8(mine, "gather_small")
    S = dict(zip(STAT_KINDS, [v.reshape(nl, r // nl, v.shape[1]) for v, r in zip(sum8(gathered, rows), rows)]))
    G = {}
    s1 = S['s1024']
    mod_rows = (0, 1, 8, 16, 17, 24)
    G['b_ada'] = jnp.concatenate([s1[:, r] for r in mod_rows], axis=1)
    G['ln1_g'], G['ln1_b'], G['ln2_g'], G['ln2_b'] = s1[:, 9], s1[:, 10], s1[:, 25], s1[:, 26]
    mst = S['mst']
    sc_g, cf_g = mst[:, 0:SC_TAPS], mst[:, MIXP_CW:MIXP_CW + CF_TAPS]
    G['sc_w'] = lax.dynamic_slice_in_dim(sc_g, chip * 64, 64, axis=2)
    G['cf_dw_w'] = lax.dynamic_slice_in_dim(cf_g, chip * 64, 64, axis=2)
    G['cf_dw_b'], G['cf_ln_g'], G['cf_ln_b'], G['pool_scale'] = mst[:, MIXP_CB], mst[:, MIXP_LG], mst[:, MIXP_LB], mst[:, MIXP_PS]
    G['pool_w'] = S['dpool'].reshape(nl, 4, 64, 64)
    G['ssm_d'], G['ssm_b_glu'] = S['vst'][:, 0], S['vst'][:, 1]
    G['b_in'] = jnp.concatenate([S['dbin'][:, 0], S['vst'][:, 2]], axis=1)
    dbc = S['dbc'].reshape(nl, SSM_G, SSM_H, 2, SSM_P)
    dcc = S['dcc'].reshape(nl, SSM_G, SSM_H, 2, SSM_P)
    G['ssm_c_re'], G['ssm_c_im'] = dcc[:, :, :, 0], -dcc[:, :, :, 1]
    da = S['da'][:, 0]
    cot = (da[:, :NST].reshape(nl, SSM_G, SSM_P), da[:, NST:].reshape(nl, SSM_G, SSM_P),
           jnp.swapaxes(dbc[:, :, :, 0], 2, 3), jnp.swapaxes(dbc[:, :, :, 1], 2, 3))
    ssm_in = ('ssm_lam_re', 'ssm_lam_im', 'ssm_log_dt', 'ssm_b_re', 'ssm_b_im')
    _, vjp = jax.vjp(_ssm_prep, *[W[n] for n in ssm_in])
    G.update(dict(zip(ssm_in, vjp(cot))))
    g1 = gathered[0].reshape(N_DEV, nl, 32, D)
    dmod_all = jnp.concatenate([g1[:, :, r] for r in mod_rows], axis=2)
    dm_cols = lax.dynamic_slice_in_dim(jnp.swapaxes(dmod_all, 0, 1), chip * (6 * D // N_CHIPS), 6 * D // N_CHIPS, axis=2)
    G['w_ada'] = ada_grad(c_all, dm_cols)
    for n in BIG:
        g = jnp.stack(gbig[n])
        G[n] = jnp.swapaxes(g, 1, 2) if n in BIG_T else g

    big_names = ('w_ada',) + BIG
    delta, new_m, new_v = {}, {}, {}
    for n in big_names:
        shp = W[n].shape
        two = (shp[0] * shp[1], shp[2])
        d_, m_, v_ = adamw(W[n].reshape(two), G[n].reshape(two), M[n].reshape(two), V[n].reshape(two), "adamw_" + n)
        delta[n], new_m[n], new_v[n] = d_.reshape(shp), m_.reshape(shp), v_.reshape(shp)
    rest = [n for n in WEIGHTS if n not in big_names]
    for n in rest:
        G[n] = G[n].reshape(W[n].shape)
    lane_view = lambda t: t.reshape(t.shape[:-2] + (-1,)) if t.shape[-1] < 64 and t.ndim == 4 else t
    ds, ms, vs_ = adamw_many(*[[lane_view(t[n]) for n in rest] for t in (W, G, M, V)])
    for n, d_, m_, v_ in zip(rest, ds, ms, vs_):
        delta[n], new_m[n], new_v[n] = d_.reshape(W[n].shape), m_.reshape(W[n].shape), v_.reshape(W[n].shape)
    return (loss, grad_x, *[G[n] for n in WEIGHTS], *[delta[n] for n in WEIGHTS],
            *[new_m[n] for n in WEIGHTS], *[new_v[n] for n in WEIGHTS])
```

```python
import math

import jax
import jax.numpy as jnp
import numpy as np
from jax import lax
from jax.experimental import pallas as pl
from jax.experimental.pallas import tpu as pltpu

F32 = jnp.float32
BF16 = jnp.bfloat16
SDS = jax.ShapeDtypeStruct

D = 1024
DEPTH = 4
GW = 256
IN_W = 7 * GW
D_FF = 2816
SC_TAPS = 3
POOL_WINDOWS = (2, 4, 8, 16)
CF_TAPS = 31
SSM_G, SSM_H, SSM_P = 16, 16, 64
NST = SSM_G * SSM_P
ALPHA = (2 * DEPTH) ** 0.25
LN_EPS = 1e-5
ADAM_LR, ADAM_B1, ADAM_B2, ADAM_EPS, ADAM_WD, ADAM_STEP = 0.001, 0.9, 0.999, 1e-08, 0.01, 10

TR = 512
TK = 1024
HALO = 32
FF_BLK = 1408
VMEM_LIMIT = 56 * 1024 * 1024
N_DEV = 8
N_CHIPS = 4

MIXP_ROWS = 56
MIXP_CW, MIXP_CB, MIXP_LG, MIXP_LB, MIXP_PS = 8, 40, 41, 42, 48


def _cp(n_axes):
    return pltpu.CompilerParams(dimension_semantics=("arbitrary",) * n_axes, vmem_limit_bytes=VMEM_LIMIT)


def _sigmoid(x):
    return 0.5 * jnp.tanh(0.5 * x) + 0.5


_GELU_C = math.sqrt(2.0 / math.pi)


def _gelu(x):
    t = jnp.tanh(_GELU_C * (x + 0.044715 * (x * x * x)))
    return 0.5 * x * (1.0 + t), t


def _gelu_grad(x, t):
    return 0.5 * (1.0 + t) + 0.5 * x * (1.0 - t * t) * (_GELU_C * (1.0 + 3 * 0.044715 * (x * x)))


def _ln_stats(r):
    mu = jnp.mean(r, axis=-1, keepdims=True)
    xc = r - mu
    var = jnp.mean(xc * xc, axis=-1, keepdims=True)
    rstd = lax.rsqrt(var + LN_EPS)
    return xc * rstd, rstd


def _ln_bwd(dy, r, gamma):
    xhat, rstd = _ln_stats(r)
    dxh = dy * gamma
    m1 = jnp.mean(dxh, axis=-1, keepdims=True)
    m2 = jnp.mean(dxh * xhat, axis=-1, keepdims=True)
    return rstd * (dxh - m1 - xhat * m2), dy * xhat


def _rowsum(v):
    return jnp.sum(v, axis=0, keepdims=True)


def _dot(a, b):
    return jnp.dot(a, b, preferred_element_type=F32)


def _dot_nt(a, b):
    return lax.dot_general(a, b, (((1,), (1,)), ((), ())), preferred_element_type=F32)


def _dot_tn(a, b):
    return lax.dot_general(a, b, (((0,), (0,)), ((), ())), preferred_element_type=F32)


def _full(shape):
    return pl.BlockSpec(shape, lambda *_: (0,) * len(shape))


def mm_in_fwd(x, mod, w_t, b):
    s = x.shape[0]

    def body(x_ref, mod_ref, w_ref, b_ref, z_ref, h_ref):
        h = x_ref[...] * (1.0 + mod_ref[1:2, :]) + mod_ref[0:1, :]
        hb = h.astype(BF16)
        h_ref[...] = hb
        z_ref[...] = _dot_nt(hb, w_ref[...]) + b_ref[...]

    return pl.pallas_call(
        body, name="mm_in_fwd", grid=(s // TR,),
        in_specs=[pl.BlockSpec((TR, D), lambda i: (i, 0)), _full((8, D)), _full((IN_W, D)), _full((1, IN_W))],
        out_specs=[pl.BlockSpec((TR, IN_W), lambda i: (i, 0)), pl.BlockSpec((TR, D), lambda i: (i, 0))],
        out_shape=[SDS((s, IN_W), F32), SDS((s, D), BF16)], compiler_params=_cp(1),
    )(x, mod, w_t, b)


def _pool_consts(i, rows):
    lane = lax.broadcasted_iota(jnp.int32, (1, GW), 1) // (GW // 4)
    wl = jnp.where(lane == 0, 2.0, jnp.where(lane == 1, 4.0, jnp.where(lane == 2, 8.0, 16.0))).astype(F32)
    pos = (i * TR + 1 + lax.broadcasted_iota(jnp.int32, (rows, 1), 0)).astype(F32)
    return lane, jnp.minimum(pos, wl)


def _pick_window(lane, c2, c4, c8, c16):
    return jnp.where(lane == 0, c2, jnp.where(lane == 1, c4, jnp.where(lane == 2, c8, c16)))


CH = 64
PADR = 8


def _fill_shifted(src, col0, dst):
    rows = dst.shape[1]
    for r in range(8):
        dst[r] = src[pl.ds(r, rows), col0:col0 + GW]


def _tap(cp, start, r0, n=CH):
    return cp[start % 8, pl.ds(start - start % 8 + r0, n), :]


def mix_fwd(z, mixp, pblk):
    s = z.shape[0]
    nt = s // TR
    r = TR // HALO
    ext = HALO + TR

    def body(z_ref, zp_ref, p_ref, pblk_ref, y_ref, hc_ref, dms, buf, cpp, cpc):
        i = pl.program_id(0)
        zp = jnp.where(i > 0, zp_ref[...], 0.0)
        buf[ext:ext + PADR, :] = jnp.zeros((PADR, 768), F32)
        buf[0:HALO, 0:256] = zp[:, 512:768] * zp[:, 0:256]
        buf[HALO:ext, 0:256] = z_ref[:, 512:768] * z_ref[:, 0:256]
        buf[0:HALO, 256:512] = zp[:, 768:1024]
        buf[HALO:ext, 256:512] = z_ref[:, 768:1024]
        buf[0:HALO, 512:768] = zp[:, 1024:1280] * _sigmoid(zp[:, 1280:1536])
        buf[HALO:ext, 512:768] = z_ref[:, 1024:1280] * _sigmoid(z_ref[:, 1280:1536])
        _fill_shifted(buf, 256, cpp)
        _fill_shifted(buf, 512, cpc)
        lane, cnt = _pool_consts(i, TR)
        for r0 in range(0, TR, CH):
            rows = slice(r0, r0 + CH)
            cv = p_ref[0:1, :] * buf[pl.ds(HALO - 2 + r0, CH), 0:256]
            for j in range(1, SC_TAPS):
                cv = cv + p_ref[j:j + 1, :] * buf[pl.ds(HALO - 2 + j + r0, CH), 0:256]
            y_ref[rows, 0:256] = (z_ref[rows, 256:512] * cv).astype(BF16)
            acc = _tap(cpp, HALO, r0)
            caps = {}
            for k in range(1, 16):
                acc = acc + _tap(cpp, HALO - k, r0)
                if k + 1 in POOL_WINDOWS:
                    caps[k + 1] = acc
            dmean = _pick_window(lane, caps[2], caps[4], caps[8], caps[16]) / cnt[rows, :] - z_ref[rows, 768:1024]
            dms[rows, :] = dmean.astype(BF16)
            hc = p_ref[MIXP_CW:MIXP_CW + 1, :] * _tap(cpc, HALO - 30, r0)
            for j in range(1, CF_TAPS):
                hc = hc + p_ref[MIXP_CW + j:MIXP_CW + j + 1, :] * _tap(cpc, HALO - 30 + j, r0)
            hc = hc + p_ref[MIXP_CB:MIXP_CB + 1, :]
            hc_ref[rows, :] = hc
            xhat, _ = _ln_stats(hc)
            hn = xhat * p_ref[MIXP_LG:MIXP_LG + 1, :] + p_ref[MIXP_LB:MIXP_LB + 1, :]
            y_ref[rows, 512:768] = (hn * _sigmoid(hn)).astype(BF16)
        y_ref[:, 256:512] = (_dot(dms[...], pblk_ref[...]) * p_ref[MIXP_PS:MIXP_PS + 1, :]).astype(BF16)

    tile = pl.BlockSpec((TR, GW), lambda i: (i, 0))
    return pl.pallas_call(
        body, name="mix_fwd", grid=(nt,),
        in_specs=[pl.BlockSpec((TR, 1536), lambda i: (i, 0)),
                  pl.BlockSpec((HALO, 1536), lambda i: (jnp.maximum(i * r - 1, 0), 0)),
                  _full((MIXP_ROWS, GW)), _full((GW, GW))],
        out_specs=[pl.BlockSpec((TR, 768), lambda i: (i, 0)), tile, tile],
        out_shape=[SDS((s, D), BF16), SDS((s, GW), F32), SDS((s, GW), BF16)],
        scratch_shapes=[pltpu.VMEM((ext + PADR, 768), F32), pltpu.VMEM((8, ext, GW), F32), pltpu.VMEM((8, ext, GW), F32)],
        compiler_params=_cp(1),
    )(z, z, mixp, pblk)


SCAN_LANE_TILES = 8


def _scan_rows(st, base, nrows, tab_ref, carry_ref, reverse):
    ng = nrows // 8
    edge = slice(0, 1) if reverse else slice(7, 8)
    for j0 in range(0, NST // 128, SCAN_LANE_TILES):
        cols = [(slice(j * 128, (j + 1) * 128), slice(NST + j * 128, NST + (j + 1) * 128))
                for j in range(j0, j0 + SCAN_LANE_TILES)]

        def group(gi, carry, cols=cols):
            g = (ng - 1 - gi) if reverse else gi
            r0 = pl.multiple_of(base + g * 8, 8)
            out = []
            for (cr, ci), (c_r, c_i) in zip(cols, carry):
                xr = st[pl.ds(r0, 8), cr]
                xi = st[pl.ds(r0, 8), ci]
                for k, sft in enumerate((1, 2, 4)):
                    ar, ai = tab_ref[8 * k:8 * k + 8, cr], tab_ref[8 * k:8 * k + 8, ci]
                    amt = (8 - sft) if reverse else sft
                    rr = pltpu.roll(xr, amt, 0)
                    ri = pltpu.roll(xi, amt, 0)
                    xr, xi = xr + (ar * rr - ai * ri), xi + (ar * ri + ai * rr)
                pr, pi = tab_ref[24:32, cr], tab_ref[24:32, ci]
                xr, xi = xr + (pr * c_r - pi * c_i), xi + (pr * c_i + pi * c_r)
                st[pl.ds(r0, 8), cr] = xr
                st[pl.ds(r0, 8), ci] = xi
                out.append((jnp.broadcast_to(xr[edge, :], (8, 128)), jnp.broadcast_to(xi[edge, :], (8, 128))))
            return tuple(out)

        res = lax.fori_loop(0, ng, group, tuple((carry_ref[:, cr], carry_ref[:, ci]) for cr, ci in cols))
        for (cr, ci), (c_r, c_i) in zip(cols, res):
            carry_ref[:, cr] = c_r
            carry_ref[:, ci] = c_i


def ssm_fwd(z, ymix, bblk, cblk, tabf, ssmv, wglu):
    s = z.shape[0]
    nt = s // TR

    def body(u_ref, ymix_in, bblk_ref, cblk_ref, tab_ref, v_ref, wglu_ref, yd_ref, y_ref, xb_ref, st, carry):
        i = pl.program_id(0)

        @pl.when(i == 0)
        def _():
            carry[...] = jnp.zeros_like(carry)

        xb_ref[...] = carry[...]
        u = u_ref[...]
        st[...] = _dot(u.astype(BF16), bblk_ref[...])
        _scan_rows(st, 0, TR, tab_ref, carry, reverse=False)
        y = _dot(st[...].astype(BF16), cblk_ref[...]) + v_ref[0:1, :] * u
        y_ref[...] = y
        yg, _ = _gelu(y)
        q = _dot(yg.astype(BF16), wglu_ref[...]) + v_ref[1:2, :]
        yd_ref[...] = (yg * _sigmoid(q)).astype(BF16)

    return pl.pallas_call(
        body, name="ssm_fwd", grid=(nt,),
        in_specs=[pl.BlockSpec((TR, GW), lambda i: (i, 6)), pl.BlockSpec(memory_space=pl.ANY),
                  _full((GW, 2 * NST)), _full((2 * NST, GW)), _full((32, 2 * NST)), _full((8, GW)), _full((GW, GW))],
        out_specs=[pl.BlockSpec((TR, GW), lambda i: (i, 3)), pl.BlockSpec((TR, GW), lambda i: (i, 0)),
                   pl.BlockSpec((8, 2 * NST), lambda i: (i, 0))],
        out_shape=[SDS((s, D), BF16), SDS((s, GW), F32), SDS((nt * 8, 2 * NST), F32)],
        scratch_shapes=[pltpu.VMEM((TR, 2 * NST), F32), pltpu.VMEM((8, 2 * NST), F32)],
        input_output_aliases={1: 0}, compiler_params=_cp(1),
    )(z, ymix, bblk, cblk, tabf, ssmv, wglu)


def mm_res_ln(a, w, xres, mod, lnp, g_row, ln_row, h2_rows):
    s, k = a.shape

    def body(a_ref, w_ref, x_ref, mod_ref, ln_ref, f_ref, r_ref, xo_ref, *h_ref):
        f = _dot(a_ref[...], w_ref[...])
        f_ref[...] = f.astype(BF16)
        r = ALPHA * x_ref[...] + (1.0 + mod_ref[g_row:g_row + 1, :]) * f
        r_ref[...] = r
        xhat, _ = _ln_stats(r)
        xo = xhat * ln_ref[ln_row:ln_row + 1, :] + ln_ref[ln_row + 1:ln_row + 2, :]
        xo_ref[...] = xo
        if h2_rows is not None:
            sh, sc = h2_rows
            h_ref[0][...] = (xo * (1.0 + mod_ref[sc:sc + 1, :]) + mod_ref[sh:sh + 1, :]).astype(BF16)

    row = pl.BlockSpec((TR, D), lambda i: (i, 0))
    outs = [SDS((s, D), BF16)] + [SDS((s, D), F32)] * 2 + ([SDS((s, D), BF16)] if h2_rows is not None else [])
    return pl.pallas_call(
        body, name="mm_res_ln_%d" % k, grid=(s // TR,),
        in_specs=[pl.BlockSpec((TR, k), lambda i: (i, 0)), _full((k, D)), row, _full((8, D)), _full((8, D))],
        out_specs=[row] * len(outs), out_shape=outs, compiler_params=_cp(1),
    )(a, w, xres, mod, lnp)


def mm_gate_up(h2, wg_t, wu_t, carry=()):
    s = h2.shape[0]
    nn, nt = D_FF // FF_BLK, s // TR
    na = len(carry)

    def body(*refs):
        h_ref, wg_ref, wu_ref = refs[:3]
        gt_ref, up_ref, a_ref = refs[3 + na:6 + na]
        if na:
            step = pl.program_id(0) * nt + pl.program_id(1)
            start, forward, finish = _gather_phases(refs[3:3 + na], refs[6 + na:6 + 2 * na], *refs[6 + 2 * na:])
            pl.when(step == 0)(start)
        h = h_ref[...]
        gt = _dot_nt(h, wg_ref[...])
        up = _dot_nt(h, wu_ref[...])
        gt_ref[...] = gt.astype(BF16)
        up_ref[...] = up.astype(BF16)
        a_ref[...] = (gt * _sigmoid(gt) * up).astype(BF16)
        if na:
            pl.when(step == nn * nt - 1 - max(1, nt // 4))(forward)
            pl.when(step == nn * nt - 1)(finish)

    wspec = pl.BlockSpec((FF_BLK, D), lambda n, i: (n, 0))
    ospec = pl.BlockSpec((TR, FF_BLK), lambda n, i: (i, n))
    sems = [pltpu.SemaphoreType.DMA((7 * na,)), pltpu.SemaphoreType.DMA((7 * na,)), pltpu.SemaphoreType.DMA((na,))]
    return pl.pallas_call(
        body, name="mm_gate_up_carry" if na else "mm_gate_up", grid=(nn, nt),
        in_specs=[pl.BlockSpec((TR, D), lambda n, i: (i, 0)), wspec, wspec] + [ANY] * na,
        out_specs=[ospec, ospec, ospec] + [ANY] * na,
        out_shape=[SDS((s, D_FF), BF16)] * 3 + [SDS((N_DEV * v.shape[0], v.shape[1]), v.dtype) for v in carry],
        scratch_shapes=sems if na else [], compiler_params=_cp(2),
    )(h2, wg_t, wu_t, *carry)


def loss_fwd(x, target):
    s = x.shape[0]

    def body(x_ref, t_ref, l_ref, dx_ref):
        i = pl.program_id(0)

        @pl.when(i == 0)
        def _():
            l_ref[...] = jnp.zeros_like(l_ref)

        e = x_ref[...] - t_ref[...]
        dx_ref[...] = e * (1.0 / D)
        part = jnp.sum(jnp.mean(e * e, axis=-1, keepdims=True), axis=0, keepdims=True)
        l_ref[...] = l_ref[...] + 0.5 * part

    row = pl.BlockSpec((TR, D), lambda i: (i, 0))
    return pl.pallas_call(
        body, name="loss_fwd", grid=(s // TR,), in_specs=[row, row], out_specs=[_full((8, 128)), row],
        out_shape=[SDS((8, 128), F32), SDS((s, D), F32)], compiler_params=_cp(1),
    )(x, target)


def bwd_ln_o(dx, r, y, w_o, mod, lnp):
    s = dx.shape[0]

    def body(dx_ref, r_ref, y_ref, w_ref, mod_ref, ln_ref, dr_ref, dy_ref, dm_ref, st_ref):
        i = pl.program_id(0)

        @pl.when(i == 0)
        def _():
            st_ref[...] = jnp.zeros_like(st_ref)

        dx = dx_ref[...]
        dr, dg = _ln_bwd(dx, r_ref[...], ln_ref[0:1, :])
        dr_ref[...] = dr
        dy = ((1.0 + mod_ref[2:3, :]) * dr).astype(BF16)
        dy_ref[...] = dy
        dm_ref[...] = _dot_nt(dy, w_ref[...])
        st_ref[0:1, :] += _rowsum(dr * y_ref[...].astype(F32))
        st_ref[1:2, :] += _rowsum(dg)
        st_ref[2:3, :] += _rowsum(dx)

    row = pl.BlockSpec((TR, D), lambda i: (i, 0))
    return pl.pallas_call(
        body, name="bwd_ln_o", grid=(s // TR,),
        in_specs=[row, row, row, _full((D, D)), _full((8, D)), _full((8, D))],
        out_specs=[row, row, row, _full((8, D))],
        out_shape=[SDS((s, D), F32), SDS((s, D), BF16), SDS((s, D), F32), SDS((8, D), F32)], compiler_params=_cp(1),
    )(dx, r, y, w_o, mod, lnp)


def bwd_ln2(dx, r, f, mod, lnp):
    s = dx.shape[0]

    def body(dx_ref, r_ref, f_ref, mod_ref, ln_ref, dr_ref, df_ref, st_ref):
        @pl.when(pl.program_id(0) == 0)
        def _():
            st_ref[...] = jnp.zeros_like(st_ref)

        dx = dx_ref[...]
        dr, dg = _ln_bwd(dx, r_ref[...], ln_ref[2:3, :])
        dr_ref[...] = dr
        df_ref[...] = ((1.0 + mod_ref[5:6, :]) * dr).astype(BF16)
        st_ref[0:1, :] += _rowsum(dr * f_ref[...].astype(F32))
        st_ref[1:2, :] += _rowsum(dg)
        st_ref[2:3, :] += _rowsum(dx)

    row = pl.BlockSpec((TR, D), lambda i: (i, 0))
    return pl.pallas_call(
        body, name="bwd_ln2", grid=(s // TR,), in_specs=[row, row, row, _full((8, D)), _full((8, D))],
        out_specs=[row, row, _full((8, D))], out_shape=[SDS((s, D), F32), SDS((s, D), BF16), SDS((8, D), F32)],
        compiler_params=_cp(1),
    )(dx, r, f, mod, lnp)


def _carry_specs(carry):
    na = len(carry)
    sems = [pltpu.SemaphoreType.DMA((3 * na,)), pltpu.SemaphoreType.DMA((3 * na,))] if na else []
    return [ANY] * na, [SDS((3,) + p.shape[1:], p.dtype) for p in carry], sems


def bwd_swiglu(df, gt, up, w_down, carry=()):
    s = df.shape[0]
    nn, nt = D_FF // FF_BLK, s // TR
    na = len(carry)

    def body(*refs):
        df_ref, gt_ref, up_ref, w_ref = refs[:4]
        dgt_ref, dup_ref = refs[4 + na:6 + na]
        n, i = pl.program_id(0), pl.program_id(1)
        if na:
            start, finish = _scatter_phases(refs[4:4 + na], refs[6 + na:6 + 2 * na], *refs[6 + 2 * na:])
            pl.when((n == 0) & (i == 0))(start)
        da = _dot_nt(df_ref[...], w_ref[...])
        gt = gt_ref[...].astype(F32)
        sg = _sigmoid(gt)
        dgt_ref[...] = (da * up_ref[...].astype(F32) * (sg * (1.0 + gt * (1.0 - sg)))).astype(BF16)
        dup_ref[...] = (da * (gt * sg)).astype(BF16)
        if na:
            pl.when((n == nn - 1) & (i == nt - 1))(finish)

    ff = pl.BlockSpec((TR, FF_BLK), lambda n, i: (i, n))
    c_in, c_out, sems = _carry_specs(carry)
    return pl.pallas_call(
        body, name="bwd_swiglu_carry" if na else "bwd_swiglu", grid=(nn, nt),
        in_specs=[pl.BlockSpec((TR, D), lambda n, i: (i, 0)), ff, ff, pl.BlockSpec((FF_BLK, D), lambda n, i: (n, 0))] + c_in,
        out_specs=[ff, ff] + c_in, out_shape=[SDS((s, D_FF), BF16)] * 2 + c_out,
        scratch_shapes=sems, compiler_params=_cp(2),
    )(df, gt, up, w_down, *carry)


def bwd_dx_mod(parts, dr, xin, mod, sh_row, sc_row, name, carry=()):
    s = dr.shape[0]
    nt = s // TR
    npart = len(parts)
    nin = 2 * npart + 3
    na = len(carry)

    def body(*refs):
        ins, (dx_ref, st_ref) = refs[:nin], refs[nin + na:nin + na + 2]
        dr_ref, x_ref, mod_ref = ins[2 * npart:]
        i = pl.program_id(0)
        if na:
            start, finish = _scatter_phases(refs[nin:nin + na], refs[nin + na + 2:nin + 2 * na + 2], *refs[nin + 2 * na + 2:])
            pl.when(i == 0)(start)

        @pl.when(i == 0)
        def _():
            st_ref[...] = jnp.zeros_like(st_ref)

        dh = _dot(ins[0][...], ins[1][...])
        for k in range(1, npart):
            dh = dh + _dot(ins[2 * k][...], ins[2 * k + 1][...])
        dx_ref[...] = ALPHA * dr_ref[...] + dh * (1.0 + mod_ref[sc_row:sc_row + 1, :])
        st_ref[0:1, :] += _rowsum(dh)
        st_ref[1:2, :] += _rowsum(dh * x_ref[...])
        if na:
            pl.when(i == nt - 1)(finish)

    row = pl.BlockSpec((TR, D), lambda i: (i, 0))
    in_specs, args = [], []
    for g, w in parts:
        kk = g.shape[1]
        in_specs += [pl.BlockSpec((TR, kk), lambda i: (i, 0)), _full((kk, D))]
        args += [g, w]
    c_in, c_out, sems = _carry_specs(carry)
    return pl.pallas_call(
        body, name=name + ("_carry" if na else ""), grid=(nt,), in_specs=in_specs + [row, row, _full((8, D))] + c_in,
        out_specs=[row, _full((8, D))] + c_in, out_shape=[SDS((s, D), F32), SDS((8, D), F32)] + c_out,
        scratch_shapes=sems, compiler_params=_cp(1),
    )(*args, dr, xin, mod, *carry)


def mm_tn(a, b, tm, name):
    s, m = a.shape
    n = b.shape[1]
    tk = min(TK, s)

    def body(a_ref, b_ref, o_ref):
        @pl.when(pl.program_id(1) == 0)
        def _():
            o_ref[...] = jnp.zeros_like(o_ref)

        o_ref[...] += _dot_tn(a_ref[...], b_ref[...])

    return pl.pallas_call(
        body, name=name, grid=(m // tm, s // tk),
        in_specs=[pl.BlockSpec((tk, tm), lambda j, k: (k, j)), pl.BlockSpec((tk, n), lambda j, k: (k, 0))],
        out_specs=pl.BlockSpec((tm, n), lambda j, k: (j, 0)), out_shape=SDS((m, n), F32), compiler_params=_cp(2),
    )(a, b)


def mix_bwd(z, dymix, hc, dmean, mixp, pblk):
    s = z.shape[0]
    nt = s // TR
    r = TR // HALO
    ext = TR + HALO

    def body(z_ref, zp_ref, zn_ref, dy_ref, dyn_ref, hc_ref, hcn_ref, dms, p_ref, pblk_ref,
             dz_ref, st_ref, dp_ref, db_ref, fbuf, bbuf, cbp, cfc, cbc, dos, dds, dpacc):
        i = pl.program_id(0)

        @pl.when(i == 0)
        def _():
            st_ref[...] = jnp.zeros_like(st_ref)
            dpacc[...] = jnp.zeros_like(dpacc)
            db_ref[...] = jnp.zeros_like(db_ref)

        zp = jnp.where(i > 0, zp_ref[...], 0.0)
        dyn = jnp.where(i < nt - 1, dyn_ref[...], 0.0)
        zeros = jnp.zeros((PADR, 768), F32)
        fbuf[ext:ext + PADR, :] = zeros
        bbuf[ext:ext + PADR, :] = zeros
        fbuf[0:HALO, 0:256] = zp[:, 512:768] * zp[:, 0:256]
        fbuf[HALO:ext, 0:256] = z_ref[:, 512:768] * z_ref[:, 0:256]
        fbuf[0:HALO, 512:768] = zp[:, 1024:1280] * _sigmoid(zp[:, 1280:1536])
        fbuf[HALO:ext, 512:768] = z_ref[:, 1024:1280] * _sigmoid(z_ref[:, 1280:1536])
        _fill_shifted(fbuf, 512, cfc)
        lane, cnt = _pool_consts(i, ext)
        ps = p_ref[MIXP_PS:MIXP_PS + 1, :]
        gam, bet = p_ref[MIXP_LG:MIXP_LG + 1, :], p_ref[MIXP_LB:MIXP_LB + 1, :]

        bbuf[TR:ext, 0:256] = dyn[:, 0:256] * zn_ref[...]
        bbuf[0:TR, 0:256] = dy_ref[:, 0:256] * z_ref[:, 256:512]
        dos[0:TR, :] = (dy_ref[:, 256:512] * ps).astype(BF16)
        dos[TR:ext, :] = (dyn[:, 256:512] * ps).astype(BF16)
        o = _dot(dms[...], pblk_ref[...])
        st_ref[MIXP_PS:MIXP_PS + 1, :] += _rowsum(dy_ref[:, 256:512] * o)
        dd = _dot_nt(dos[...], pblk_ref[...])
        dds[...] = dd[0:TR, :]
        bbuf[0:ext, 256:512] = dd / cnt
        dpacc[...] += _dot_tn(dms[...], dos[0:TR, :])
        _fill_shifted(bbuf, 256, cbp)

        s_g, s_b, s_c = (jnp.zeros((1, GW), F32),) * 3
        for r0 in range(0, ext, HALO):
            rows = slice(r0, r0 + HALO)
            if r0 < TR:
                hce, dye = hc_ref[rows, :], dy_ref[rows, 512:768]
            else:
                hce, dye = hcn_ref[...], dyn[:, 512:768]
            xhat, rstd = _ln_stats(hce)
            hn = xhat * gam + bet
            sh = _sigmoid(hn)
            dhn = dye * (sh * (1.0 + hn * (1.0 - sh)))
            dxh = dhn * gam
            m1 = jnp.mean(dxh, axis=-1, keepdims=True)
            m2 = jnp.mean(dxh * xhat, axis=-1, keepdims=True)
            dhc = rstd * (dxh - m1 - xhat * m2)
            bbuf[rows, 512:768] = dhc
            if r0 < TR:
                s_g, s_b, s_c = s_g + _rowsum(dhn * xhat), s_b + _rowsum(dhn), s_c + _rowsum(dhc)
        st_ref[MIXP_LG:MIXP_LG + 1, :] += s_g
        st_ref[MIXP_LB:MIXP_LB + 1, :] += s_b
        st_ref[MIXP_CB:MIXP_CB + 1, :] += s_c
        _fill_shifted(bbuf, 512, cbc)

        sums = [jnp.zeros((1, GW), F32)] * 6
        for r0 in range(0, TR, CH):
            rows = slice(r0, r0 + CH)
            cv = p_ref[0:1, :] * fbuf[pl.ds(HALO - 2 + r0, CH), 0:256]
            da = p_ref[0:1, :] * bbuf[pl.ds(2 + r0, CH), 0:256]
            for j in range(1, SC_TAPS):
                cv = cv + p_ref[j:j + 1, :] * fbuf[pl.ds(HALO - 2 + j + r0, CH), 0:256]
                da = da + p_ref[j:j + 1, :] * bbuf[pl.ds(2 - j + r0, CH), 0:256]
            dz_h = da * z_ref[rows, 512:768]
            dz_b = dy_ref[rows, 0:256] * cv
            dz_c = da * z_ref[rows, 0:256]
            acc = _tap(cbp, 0, r0)
            caps = {}
            for k in range(1, 16):
                acc = acc + _tap(cbp, k, r0)
                if k + 1 in POOL_WINDOWS:
                    caps[k + 1] = acc
            dz_p = _pick_window(lane, caps[2], caps[4], caps[8], caps[16]) - dds[rows, :]
            dhg = p_ref[MIXP_CW:MIXP_CW + 1, :] * _tap(cbc, 30, r0)
            for j in range(1, CF_TAPS):
                dhg = dhg + p_ref[MIXP_CW + j:MIXP_CW + j + 1, :] * _tap(cbc, 30 - j, r0)
            sg = _sigmoid(z_ref[rows, 1280:1536])
            dz_v = dhg * sg
            dz_g = dhg * z_ref[rows, 1024:1280] * (sg * (1.0 - sg))
            parts = (dz_h, dz_b, dz_c, dz_p, dz_v, dz_g)
            for k, v in enumerate(parts):
                dz_ref[rows, 256 * k:256 * k + 256] = v.astype(BF16)
            sums = [a + _rowsum(v) for a, v in zip(sums, parts)]
        for k in range(6):
            db_ref[0:1, 256 * k:256 * k + 256] += sums[k]

        for j in range(CF_TAPS):
            acc = bbuf[0:CH, 512:768] * _tap(cfc, HALO - 30 + j, 0)
            for r0 in range(CH, TR, CH):
                acc = acc + bbuf[r0:r0 + CH, 512:768] * _tap(cfc, HALO - 30 + j, r0)
            st_ref[MIXP_CW + j:MIXP_CW + j + 1, :] += _rowsum(acc)
        for j in range(SC_TAPS):
            st_ref[j:j + 1, :] += _rowsum(bbuf[0:TR, 0:256] * fbuf[pl.ds(HALO - 2 + j, TR), 0:256])

        @pl.when(i == nt - 1)
        def _():
            for k in range(4):
                dp_ref[64 * k:64 * k + 64, :] = dpacc[64 * k:64 * k + 64, 64 * k:64 * k + 64]

    nxt = lambda i: jnp.minimum((i + 1) * r, nt * r - 1)
    tile = pl.BlockSpec((TR, GW), lambda i: (i, 0))
    cp = pltpu.VMEM((8, ext, GW), F32)
    return pl.pallas_call(
        body, name="mix_bwd", grid=(nt,),
        in_specs=[pl.BlockSpec((TR, 1536), lambda i: (i, 0)),
                  pl.BlockSpec((HALO, 1536), lambda i: (jnp.maximum(i * r - 1, 0), 0)),
                  pl.BlockSpec((HALO, GW), lambda i: (nxt(i), 1)),
                  pl.BlockSpec((TR, 768), lambda i: (i, 0)), pl.BlockSpec((HALO, 768), lambda i: (nxt(i), 0)),
                  tile, pl.BlockSpec((HALO, GW), lambda i: (nxt(i), 0)), tile, _full((MIXP_ROWS, GW)), _full((GW, GW))],
        out_specs=[pl.BlockSpec((TR, 1536), lambda i: (i, 0)), _full((MIXP_ROWS, GW)), _full((GW, 64)), _full((8, 1536))],
        out_shape=[SDS((s, IN_W), BF16), SDS((MIXP_ROWS, GW), F32), SDS((GW, 64), F32), SDS((8, 1536), F32)],
        scratch_shapes=[pltpu.VMEM((ext + PADR, 768), F32), pltpu.VMEM((ext + PADR, 768), F32), cp, cp, cp,
                        pltpu.VMEM((ext, GW), BF16), pltpu.VMEM((TR, GW), F32), pltpu.VMEM((GW, GW), F32)],
        compiler_params=_cp(1),
    )(z, z, z, dymix, dymix, hc, hc, dmean, mixp, pblk)


def ssm_bwd(z, dz, dymix, y, xb, bblk, cblk, tabf, tabb, ssmv, wglu):
    s = z.shape[0]
    nt = s // TR

    def body(u_ref, dzin, dyd_ref, y_ref, xb_ref, bblk_ref, cblk_ref, tabf_ref, tabb_ref, v_ref, wglu_ref,
             dzs_ref, dbc_ref, dcc_ref, da_ref, dwg_ref, vst_ref, st, gs, carry, gcarry, dbacc, dcacc):
        i = pl.program_id(0)

        @pl.when(i == 0)
        def _():
            gcarry[...] = jnp.zeros_like(gcarry)
            for ref in (dbacc, dcacc, da_ref, dwg_ref, vst_ref):
                ref[...] = jnp.zeros_like(ref)

        u = u_ref[...]
        y = y_ref[...]
        yg, th = _gelu(y)
        ygb = yg.astype(BF16)
        q = _dot(ygb, wglu_ref[...]) + v_ref[1:2, :]
        sq = _sigmoid(q)
        dout = dyd_ref[...]
        dq = dout * yg * (sq * (1.0 - sq))
        dqb = dq.astype(BF16)
        dyg = dout * sq + _dot_nt(dqb, wglu_ref[...])
        dy = dyg * _gelu_grad(y, th)
        dyb = dy.astype(BF16)
        dwg_ref[...] += _dot_tn(ygb, dqb)
        vst_ref[0:1, :] += _rowsum(dy * u)
        vst_ref[1:2, :] += _rowsum(dq)
        ub = u.astype(BF16)
        carry[...] = xb_ref[...]
        st[0:8, :] = xb_ref[...]
        st[8:8 + TR, :] = _dot(ub, bblk_ref[...])
        _scan_rows(st, 8, TR, tabf_ref, carry, reverse=False)
        gs[...] = _dot_nt(dyb, cblk_ref[...])
        _scan_rows(gs, 0, TR, tabb_ref, gcarry, reverse=True)
        xs = st[pl.ds(8, TR), :]
        dcacc[...] += _dot_tn(dyb, xs.astype(BF16))
        g = gs[...]
        gb = g.astype(BF16)
        dbacc[...] += _dot_tn(ub, gb)
        gr, gi = g[:, 0:NST], g[:, NST:]
        xp = st[pl.ds(7, TR), :]
        xr, xi = xp[:, 0:NST], xp[:, NST:]
        da_ref[0:1, 0:NST] += _rowsum(gr * xr + gi * xi)
        da_ref[0:1, NST:] += _rowsum(gi * xr - gr * xi)
        du = _dot_nt(gb, bblk_ref[...]) + v_ref[0:1, :] * dy
        dzs_ref[...] = du.astype(BF16)
        vst_ref[2:3, :] += _rowsum(du)

        @pl.when(i == nt - 1)
        def _():
            for g_ in range(SSM_G):
                rows = slice(g_ * SSM_H, (g_ + 1) * SSM_H)
                for acc, out in ((dbacc, dbc_ref), (dcacc, dcc_ref)):
                    out[rows, 0:SSM_P] = acc[rows, g_ * SSM_P:(g_ + 1) * SSM_P]
                    out[rows, SSM_P:2 * SSM_P] = acc[rows, NST + g_ * SSM_P:NST + (g_ + 1) * SSM_P]

    rev = lambda i: nt - 1 - i
    return pl.pallas_call(
        body, name="ssm_bwd", grid=(nt,),
        in_specs=[pl.BlockSpec((TR, GW), lambda i: (rev(i), 6)), pl.BlockSpec(memory_space=pl.ANY),
                  pl.BlockSpec((TR, GW), lambda i: (rev(i), 3)), pl.BlockSpec((TR, GW), lambda i: (rev(i), 0)),
                  pl.BlockSpec((8, 2 * NST), lambda i: (rev(i), 0)),
                  _full((GW, 2 * NST)), _full((2 * NST, GW)), _full((32, 2 * NST)), _full((32, 2 * NST)),
                  _full((8, GW)), _full((GW, GW))],
        out_specs=[pl.BlockSpec((TR, GW), lambda i: (rev(i), 6)), _full((GW, 2 * SSM_P)), _full((GW, 2 * SSM_P)),
                   _full((8, 2 * NST)), _full((GW, GW)), _full((8, GW))],
        out_shape=[SDS((s, IN_W), BF16), SDS((GW, 2 * SSM_P), F32), SDS((GW, 2 * SSM_P), F32), SDS((8, 2 * NST), F32),
                   SDS((GW, GW), F32), SDS((8, GW), F32)],
        scratch_shapes=[pltpu.VMEM((8 + TR, 2 * NST), F32), pltpu.VMEM((TR, 2 * NST), F32),
                        pltpu.VMEM((8, 2 * NST), F32), pltpu.VMEM((8, 2 * NST), F32),
                        pltpu.VMEM((GW, 2 * NST), F32), pltpu.VMEM((GW, 2 * NST), F32)],
        input_output_aliases={1: 0}, compiler_params=_cp(1),
    )(z, dz, dymix, y, xb, bblk, cblk, tabf, tabb, ssmv, wglu)


def _ssm_prep(lam_re, lam_im, log_dt, b_re, b_im):
    dt = jnp.exp(log_dt)[..., None]
    mag = jnp.exp(lam_re * dt)
    ar, ai = mag * jnp.cos(lam_im * dt), mag * jnp.sin(lam_im * dt)
    den = lam_re * lam_re + lam_im * lam_im
    qr = ((ar - 1.0) * lam_re + ai * lam_im) / den
    qi = (ai * lam_re - (ar - 1.0) * lam_im) / den
    bbr = qr[..., None] * b_re - qi[..., None] * b_im
    bbi = qr[..., None] * b_im + qi[..., None] * b_re
    return ar, ai, bbr, bbi


def _ssm_tables(lam_re, lam_im, log_dt):
    nl = lam_re.shape[0]
    dt = jnp.exp(log_dt)[:, None, :, None]
    k = jnp.arange(1, 9, dtype=F32)[None, :, None, None]
    mag = jnp.exp(k * (lam_re[:, None] * dt))
    ang = k * (lam_im[:, None] * dt)
    pr = (mag * jnp.cos(ang)).reshape(nl, 8, NST)
    pi = (mag * jnp.sin(ang)).reshape(nl, 8, NST)
    row = jnp.arange(8)[None, :, None]

    def table(sign, reverse):
        parts_r, parts_i = [], []
        for sft in (1, 2, 4):
            keep = (row < 8 - sft) if reverse else (row >= sft)
            parts_r.append(jnp.where(keep, pr[:, sft - 1:sft], 0.0))
            parts_i.append(jnp.where(keep, sign * pi[:, sft - 1:sft], 0.0))
        parts_r.append(pr[:, ::-1] if reverse else pr)
        parts_i.append(sign * (pi[:, ::-1] if reverse else pi))
        return jnp.concatenate([jnp.concatenate(parts_r, axis=1), jnp.concatenate(parts_i, axis=1)], axis=2)

    return table(1.0, False), table(-1.0, True)


def _blockdiag(m):
    nl, g, a, b = m.shape
    return jnp.einsum('lgab,gk->lgakb', m, jnp.eye(g, dtype=m.dtype)).reshape(nl, g * a, g * b)


def _rows_at(blocks, total):
    out, at = [], 0
    nl, _, c = blocks[0][1].shape
    for r0, b in blocks:
        if r0 > at:
            out.append(jnp.zeros((nl, r0 - at, c), F32))
        out.append(b)
        at = r0 + b.shape[1]
    if total > at:
        out.append(jnp.zeros((nl, total - at, c), F32))
    return jnp.concatenate(out, axis=1)


def prep_params(p):
    ar, ai, bbr, bbi = _ssm_prep(p['ssm_lam_re'], p['ssm_lam_im'], p['ssm_log_dt'], p['ssm_b_re'], p['ssm_b_im'])
    bblk = jnp.concatenate([_blockdiag(jnp.swapaxes(bbr, 2, 3)), _blockdiag(jnp.swapaxes(bbi, 2, 3))], axis=2)
    cblk = jnp.concatenate([_blockdiag(jnp.swapaxes(p['ssm_c_re'], 2, 3)),
                            -_blockdiag(jnp.swapaxes(p['ssm_c_im'], 2, 3))], axis=1)
    tabf, tabb = _ssm_tables(p['ssm_lam_re'], p['ssm_lam_im'], p['ssm_log_dt'])
    mixp = _rows_at([(0, p['sc_w']), (MIXP_CW, p['cf_dw_w']), (MIXP_CB, p['cf_dw_b'][:, None]),
                     (MIXP_LG, p['cf_ln_g'][:, None]), (MIXP_LB, p['cf_ln_b'][:, None]),
                     (MIXP_PS, p['pool_scale'][:, None])], MIXP_ROWS)
    return dict(
        mixp=mixp, pblk=_blockdiag(p['pool_w']).astype(BF16), bblk=bblk.astype(BF16), cblk=cblk.astype(BF16),
        tabf=tabf, tabb=tabb, ssmv=_rows_at([(0, p['ssm_d'][:, None]), (1, p['ssm_b_glu'][:, None])], 8),
        lnp=_rows_at([(0, p['ln1_g'][:, None]), (1, p['ln1_b'][:, None]), (2, p['ln2_g'][:, None]),
                      (3, p['ln2_b'][:, None])], 8),
        b_in=p['b_in'][:, None],
    )


def layer_fwd(x, mod, w, q, next_halves=()):
    z, h1 = mm_in_fwd(x, mod, w['w_in'], q['b_in'])
    ymix, hc, dmean = mix_fwd(z, q['mixp'], q['pblk'])
    ymix, ys, xb = ssm_fwd(z, ymix, q['bblk'], q['cblk'], q['tabf'], q['ssmv'], w['ssm_w_glu'])
    y, r1, x1, h2 = mm_res_ln(ymix, w['w_o'], x, mod, q['lnp'], 2, 0, (3, 4))
    gt, up, act, *w_next = mm_gate_up(h2, w['w_gate'], w['w_up'], next_halves)
    f, r2, x2 = mm_res_ln(act, w['w_down'], x1, mod, q['lnp'], 5, 2, None)
    saved = dict(x=x, z=z, h1=h1, ymix=ymix, hc=hc, dmean=dmean, ys=ys, xb=xb, y=y, r1=r1, x1=x1, h2=h2, gt=gt, up=up,
                 act=act, f=f, r2=r2)
    return x2, saved, w_next


def layer_bwd(dx2, mod, w, q, sv, pending=()):
    dr2, df, st2 = bwd_ln2(dx2, sv['r2'], sv['f'], mod, q['lnp'])
    dgt, dup, *recv_a = bwd_swiglu(df, sv['gt'], sv['up'], w['w_down'], pending[:2])
    dx1, stm2, *recv_b = bwd_dx_mod([(dgt, w['w_gate']), (dup, w['w_up'])], dr2, sv['x1'], mod, 3, 4, "bwd_dx_ff",
                                    pending[2:])
    recv = recv_a + recv_b
    g_down = mm_tn(sv['act'], df, FF_BLK, "dw_down")
    g_gate = mm_tn(dgt, sv['h2'], FF_BLK, "dw_gate")
    g_up = mm_tn(dup, sv['h2'], FF_BLK, "dw_up")
    dr1, dy, dymix, st1 = bwd_ln_o(dx1, sv['r1'], sv['y'], w['w_o'], mod, q['lnp'])
    g_o = mm_tn(sv['ymix'], dy, D // 2, "dw_o")
    dz, mst, dpool, dbin = mix_bwd(sv['z'], dymix, sv['hc'], sv['dmean'], q['mixp'], q['pblk'])
    dz, dbc, dcc, da, g_glu, vst = ssm_bwd(sv['z'], dz, dymix, sv['ys'], sv['xb'], q['bblk'], q['cblk'], q['tabf'],
                                           q['tabb'], q['ssmv'], w['ssm_w_glu'])
    dx, stm1 = bwd_dx_mod([(dz, w['w_in'])], dr1, sv['x'], mod, 0, 1, "bwd_dx_in")
    g_in = mm_tn(dz, sv['h1'], IN_W // 2, "dw_in")
    big = [g_in, g_o, g_gate, g_up, g_down, g_glu]
    stats = dict(s1024=jnp.concatenate([stm1, st1, stm2, st2], axis=0), mst=mst, vst=vst, dpool=dpool,
                 dbc=dbc, dcc=dcc, da=da, dbin=dbin)
    return dx, big, stats, recv


MESH = pl.DeviceIdType.MESH
ANY = pl.BlockSpec(memory_space=pl.ANY)
VMEM_SPEC = pl.BlockSpec(memory_space=pltpu.VMEM)


def _place():
    return lax.axis_index("x"), lax.axis_index("y"), lax.axis_index("c")


def _gather_phases(x_refs, out_refs, send_sems, recv_sems, local_sems):
    na = len(x_refs)

    def parts():
        x, y, c = _place()
        chips = [(1 - x, y), (x, 1 - y), (1 - x, 1 - y)]

        def rows(a, px, py, pc):
            m_per = x_refs[a].shape[0]
            return out_refs[a].at[pl.ds((4 * px + 2 * py + pc) * m_per, m_per), :]

        def copy(a, k, block, to, src=None):
            return pltpu.make_async_remote_copy(
                src_ref=rows(a, *block) if src is None else src, dst_ref=rows(a, *block),
                send_sem=send_sems.at[7 * a + k], recv_sem=recv_sems.at[7 * a + k], device_id=to, device_id_type=MESH)

        def mine():
            return [pltpu.make_async_copy(x_refs[a], rows(a, x, y, c), local_sems.at[a]) for a in range(na)]

        def first():
            out = []
            for a in range(na):
                out.append(copy(a, 0, (x, y, c), (x, y, 1 - c), src=x_refs[a]))
                out += [copy(a, 1 + j, (x, y, c), (*chip, c), src=x_refs[a]) for j, chip in enumerate(chips)]
            return out

        def passed(j, a):
            return copy(a, 4 + j, (*chips[j], c), (x, y, 1 - c))

        return (x, y, c), chips, copy, mine, first, passed

    def start():
        _, _, _, mine, first, _ = parts()
        for cp in mine() + first():
            cp.start()

    def forward():
        me, chips, copy, _, _, passed = parts()
        for j, chip in enumerate(chips):
            for a in range(na):
                copy(a, 1 + j, (*chip, me[2]), me).wait_recv()
                passed(j, a).start()

    def finish():
        me, chips, copy, mine, first, passed = parts()
        for a in range(na):
            copy(a, 0, (me[0], me[1], 1 - me[2]), me).wait_recv()
            for j, chip in enumerate(chips):
                copy(a, 4 + j, (*chip, 1 - me[2]), me).wait_recv()
        for cp in first() + [passed(j, a) for j in range(3) for a in range(na)]:
            cp.wait_send()
        for cp in mine():
            cp.wait()

    return start, forward, finish


def _scatter_phases(ps_refs, out_refs, send_sems, recv_sems):
    na = len(ps_refs)

    def copies():
        x, y, c = _place()
        chips = [(1 - x, y), (x, 1 - y), (1 - x, 1 - y)]
        return [pltpu.make_async_remote_copy(
            src_ref=ps_refs[a].at[2 * kx + ky], dst_ref=out_refs[a].at[j], send_sem=send_sems.at[3 * a + j],
            recv_sem=recv_sems.at[3 * a + j], device_id=(kx, ky, c), device_id_type=MESH)
            for a in range(na) for j, (kx, ky) in enumerate(chips)]

    def start():
        for cp in copies():
            cp.start()

    def finish():
        for cp in copies():
            cp.wait()

    return start, finish


def allgather8(vs, name):
    na = len(vs)

    def body(*refs):
        start, forward, finish = _gather_phases(refs[:na], refs[na:2 * na], *refs[2 * na:])
        start()
        forward()
        finish()

    return pl.pallas_call(
        body, name=name, out_shape=[SDS((N_DEV * v.shape[0], v.shape[1]), v.dtype) for v in vs],
        in_specs=[VMEM_SPEC] * na, out_specs=[VMEM_SPEC] * na,
        scratch_shapes=[pltpu.SemaphoreType.DMA((7 * na,)), pltpu.SemaphoreType.DMA((7 * na,)),
                        pltpu.SemaphoreType.DMA((na,))],
        compiler_params=pltpu.CompilerParams(vmem_limit_bytes=VMEM_LIMIT),
    )(*vs)


def pair_swap(gs):
    na = len(gs)

    def body(*refs):
        g_refs, out_refs, (send_sems, recv_sems) = refs[:na], refs[na:2 * na], refs[2 * na:]
        x, y, c = _place()
        cps = [pltpu.make_async_remote_copy(
            src_ref=g_refs[a].at[j, 1 - c], dst_ref=out_refs[a].at[j], send_sem=send_sems.at[4 * a + j],
            recv_sem=recv_sems.at[4 * a + j], device_id=(x, y, 1 - c), device_id_type=MESH)
            for a in range(na) for j in range(N_CHIPS)]
        for cp in cps:
            cp.start()
        for cp in cps:
            cp.wait()

    return pl.pallas_call(
        body, name="pair_swap", out_shape=[SDS((g.shape[0],) + g.shape[2:], g.dtype) for g in gs],
        in_specs=[ANY] * na, out_specs=[ANY] * na,
        scratch_shapes=[pltpu.SemaphoreType.DMA((4 * na,)), pltpu.SemaphoreType.DMA((4 * na,))],
    )(*gs)


def chip_scatter(pss):
    na = len(pss)

    def body(*refs):
        start, finish = _scatter_phases(refs[:na], refs[na:2 * na], *refs[2 * na:])
        start()
        finish()

    return pl.pallas_call(
        body, name="chip_scatter", out_shape=[SDS((3,) + p.shape[1:], p.dtype) for p in pss],
        in_specs=[ANY] * na, out_specs=[ANY] * na,
        scratch_shapes=[pltpu.SemaphoreType.DMA((3 * na,)), pltpu.SemaphoreType.DMA((3 * na,))],
    )(*pss)


def pair_gather(ts):
    na = len(ts)

    def body(*refs):
        t_refs, out_refs, (send_sems, recv_sems) = refs[:na], refs[na:2 * na], refs[2 * na:]
        x, y, c = _place()
        cps = [pltpu.make_async_remote_copy(
            src_ref=t_refs[a].at[c], dst_ref=out_refs[a].at[c], send_sem=send_sems.at[a], recv_sem=recv_sems.at[a],
            device_id=(x, y, 1 - c), device_id_type=MESH) for a in range(na)]
        for cp in cps:
            cp.start()
        for cp in cps:
            cp.wait()

    return pl.pallas_call(
        body, name="pair_gather", out_shape=[SDS(t.shape, t.dtype) for t in ts],
        in_specs=[ANY] * na, out_specs=[ANY] * na, input_output_aliases={a: a for a in range(na)},
        scratch_shapes=[pltpu.SemaphoreType.DMA((na,)), pltpu.SemaphoreType.DMA((na,))],
    )(*ts)


def _scalar(v):
    return jnp.reshape(v, (1,)).astype(jnp.int32)


def pair_sum(gs, recvs):
    na = len(gs)
    c = lax.axis_index("c")

    def body(c_ref, *refs):
        for a in range(na):
            refs[2 * na + a][...] = (refs[a][...] + refs[na + a][...]).astype(BF16)

    in_specs = [pl.BlockSpec((None, None) + g.shape[2:], lambda j, c_ref: (j, c_ref[0], 0, 0)) for g in gs]
    in_specs += [pl.BlockSpec((None,) + r.shape[1:], lambda j, c_ref: (j, 0, 0)) for r in recvs]
    return pl.pallas_call(
        body, name="pair_sum",
        grid_spec=pltpu.PrefetchScalarGridSpec(
            num_scalar_prefetch=1, grid=(N_CHIPS,), in_specs=in_specs,
            out_specs=[pl.BlockSpec((None,) + r.shape[1:], lambda j, c_ref: (j, 0, 0)) for r in recvs]),
        out_shape=[SDS(r.shape, BF16) for r in recvs], compiler_params=_cp(1),
    )(_scalar(c), *gs, *recvs)


def sum_chips(pss, recvs):
    na = len(pss)
    xi, yi, ci = _place()

    def body(s_ref, *refs):
        for a in range(na):
            r = refs[na + a]
            acc = refs[a][...].astype(F32) + r[0].astype(F32)
            acc = acc + r[1].astype(F32)
            refs[2 * na + a][...] = acc + r[2].astype(F32)

    in_specs = [pl.BlockSpec((None,) + p.shape[1:], lambda i, s_ref: (s_ref[0], 0, 0)) for p in pss]
    in_specs += [pl.BlockSpec(r.shape, lambda i, s_ref: (0, 0, 0)) for r in recvs]
    return pl.pallas_call(
        body, name="sum_chips",
        grid_spec=pltpu.PrefetchScalarGridSpec(
            num_scalar_prefetch=1, grid=(1,), in_specs=in_specs,
            out_specs=[pl.BlockSpec((None,) + p.shape[1:], lambda i, s_ref: (s_ref[1], 0, 0)) for p in pss]),
        out_shape=[SDS((2,) + p.shape[1:], F32) for p in pss], compiler_params=_cp(1),
    )(jnp.stack([2 * xi + yi, ci]).astype(jnp.int32), *pss, *recvs)


def sum8(vs, rows):
    na = len(vs)

    def body(*refs):
        for a in range(na):
            m = rows[a]
            acc = refs[a][0:m, :]
            for d in range(1, N_DEV):
                acc = acc + refs[a][d * m:(d + 1) * m, :]
            refs[na + a][...] = acc

    return pl.pallas_call(
        body, name="sum8", out_shape=[SDS((rows[a], vs[a].shape[1]), F32) for a in range(na)],
        in_specs=[VMEM_SPEC] * na, out_specs=[VMEM_SPEC] * na,
        compiler_params=pltpu.CompilerParams(vmem_limit_bytes=VMEM_LIMIT),
    )(*vs)


def ada_fwd(c_all, w_ada):
    nl, _, n = w_ada.shape
    bn = 512

    def body(c_ref, w_ref, o_ref):
        cv = c_ref[...]
        cond = (cv * _sigmoid(cv)).astype(BF16)
        o_ref[...] = _dot(cond, w_ref[...].astype(BF16))

    return pl.pallas_call(
        body, name="ada_fwd", grid=(nl, n // bn),
        in_specs=[pl.BlockSpec((8, D), lambda l, j: (0, 0)), pl.BlockSpec((None, D, bn), lambda l, j: (l, 0, j))],
        out_specs=pl.BlockSpec((None, 8, bn), lambda l, j: (l, 0, j)), out_shape=SDS((nl, 8, n), F32),
        compiler_params=_cp(2),
    )(c_all, w_ada)


def ada_grad(c_all, dm):
    nl, _, n = dm.shape
    bn = 512

    def body(c_ref, d_ref, o_ref):
        cv = c_ref[...]
        cond = (cv * _sigmoid(cv)).astype(BF16)
        o_ref[...] = _dot_tn(cond, d_ref[...].astype(BF16))

    return pl.pallas_call(
        body, name="ada_grad", grid=(nl, n // bn),
        in_specs=[pl.BlockSpec((8, D), lambda l, j: (0, 0)), pl.BlockSpec((None, 8, bn), lambda l, j: (l, 0, j))],
        out_specs=pl.BlockSpec((None, D, bn), lambda l, j: (l, 0, j)), out_shape=SDS((nl, D, n), F32),
        compiler_params=_cp(2),
    )(c_all, dm)


def _adam_math(w, g, m, v):
    m = ADAM_B1 * m + (1.0 - ADAM_B1) * g
    v = ADAM_B2 * v + (1.0 - ADAM_B2) * (g * g)
    m_hat = m / (1.0 - ADAM_B1 ** ADAM_STEP)
    v_hat = v / (1.0 - ADAM_B2 ** ADAM_STEP)
    return -ADAM_LR * (m_hat / (jnp.sqrt(v_hat) + ADAM_EPS) + ADAM_WD * w), m, v


def adamw(w, g, m, v, name):
    rr, cc = w.shape
    tr = 256

    def body(w_ref, g_ref, m_ref, v_ref, d_ref, mo_ref, vo_ref):
        d_ref[...], mo_ref[...], vo_ref[...] = _adam_math(w_ref[...], g_ref[...], m_ref[...], v_ref[...])

    spec = pl.BlockSpec((tr, cc), lambda i: (i, 0))
    return pl.pallas_call(
        body, name=name, grid=(rr // tr,), in_specs=[spec] * 4, out_specs=[spec] * 3,
        out_shape=[SDS((rr, cc), F32)] * 3, compiler_params=_cp(1),
    )(w, g, m, v)


def adamw_many(ws, gs, ms, vs):
    na = len(ws)

    def body(*refs):
        for a in range(na):
            d, m, v = _adam_math(refs[a][...], refs[na + a][...], refs[2 * na + a][...], refs[3 * na + a][...])
            refs[4 * na + a][...] = d
            refs[5 * na + a][...] = m
            refs[6 * na + a][...] = v

    outs = [SDS(w.shape, F32) for w in ws] * 3
    res = pl.pallas_call(
        body, name="adamw_small", out_shape=outs, in_specs=[VMEM_SPEC] * (4 * na), out_specs=[VMEM_SPEC] * (3 * na),
        compiler_params=pltpu.CompilerParams(vmem_limit_bytes=VMEM_LIMIT),
    )(*ws, *gs, *ms, *vs)
    return res[:na], res[na:2 * na], res[2 * na:]


BIG = ('w_in', 'w_o', 'w_gate', 'w_up', 'w_down', 'ssm_w_glu')
BIG_T = ('w_in', 'w_gate', 'w_up')
WEIGHTS = ['w_ada', 'b_ada', 'w_in', 'b_in', 'sc_w', 'pool_w', 'pool_scale', 'cf_dw_w', 'cf_dw_b', 'cf_ln_g', 'cf_ln_b',
           'ssm_lam_re', 'ssm_lam_im', 'ssm_log_dt', 'ssm_b_re', 'ssm_b_im', 'ssm_c_re', 'ssm_c_im', 'ssm_d', 'ssm_w_glu',
           'ssm_b_glu', 'w_o', 'ln1_g', 'ln1_b', 'w_gate', 'w_up', 'w_down', 'ln2_g', 'ln2_b']
STAT_KINDS = ('s1024', 'mst', 'vst', 'dpool', 'dbc', 'dcc', 'da', 'dbin')


def _chip_blocks(g, width):
    m = g.shape[0] // N_DEV
    return g.reshape(N_CHIPS, 2, m, g.shape[1])[:, 0, :, :width]


def kernel(x, c, w_ada, b_ada, w_in, b_in, sc_w, pool_w, pool_scale, cf_dw_w, cf_dw_b, cf_ln_g, cf_ln_b, ssm_lam_re, ssm_lam_im, ssm_log_dt, ssm_b_re, ssm_b_im, ssm_c_re, ssm_c_im, ssm_d, ssm_w_glu, ssm_b_glu, w_o, ln1_g, ln1_b, w_gate, w_up, w_down, ln2_g, ln2_b, loss_target, m_w_ada, m_b_ada, m_w_in, m_b_in, m_sc_w, m_pool_w, m_pool_scale, m_cf_dw_w, m_cf_dw_b, m_cf_ln_g, m_cf_ln_b, m_ssm_lam_re, m_ssm_lam_im, m_ssm_log_dt, m_ssm_b_re, m_ssm_b_im, m_ssm_c_re, m_ssm_c_im, m_ssm_d, m_ssm_w_glu, m_ssm_b_glu, m_w_o, m_ln1_g, m_ln1_b, m_w_gate, m_w_up, m_w_down, m_ln2_g, m_ln2_b, v_w_ada, v_b_ada, v_w_in, v_b_in, v_sc_w, v_pool_w, v_pool_scale, v_cf_dw_w, v_cf_dw_b, v_cf_ln_g, v_cf_ln_b, v_ssm_lam_re, v_ssm_lam_im, v_ssm_log_dt, v_ssm_b_re, v_ssm_b_im, v_ssm_c_re, v_ssm_c_im, v_ssm_d, v_ssm_w_glu, v_ssm_b_glu, v_w_o, v_ln1_g, v_ln1_b, v_w_gate, v_w_up, v_w_down, v_ln2_g, v_ln2_b):
    a = dict(locals())
    W = {n: a[n] for n in WEIGHTS}
    M = {n: a['m_' + n] for n in WEIGHTS}
    V = {n: a['v_' + n] for n in WEIGHTS}
    xi, yi, ci = _place()
    me = 4 * xi + 2 * yi + ci
    chip = 2 * xi + yi
    nl = DEPTH

    g_c, g_sc, g_cf = allgather8(
        [jnp.pad(c, ((0, 7), (0, 0))), jnp.pad(sc_w.reshape(nl * SC_TAPS, 64), ((0, 4), (0, 0))),
         jnp.pad(cf_dw_w.reshape(nl * CF_TAPS, 64), ((0, 4), (0, 0)))], "gather_pre")
    c_all = g_c.reshape(N_DEV, 8, D)[:, 0]
    sc_full = jnp.moveaxis(_chip_blocks(g_sc, 64)[:, :nl * SC_TAPS].reshape(N_CHIPS, nl, SC_TAPS, 64), 0, 2)
    sc_full = sc_full.reshape(nl, SC_TAPS, GW)
    cf_full = jnp.moveaxis(_chip_blocks(g_cf, 64)[:, :nl * CF_TAPS].reshape(N_CHIPS, nl, CF_TAPS, 64), 0, 2)
    cf_full = cf_full.reshape(nl, CF_TAPS, GW)

    mod_part = ada_fwd(c_all, w_ada)
    (g_mod,) = allgather8([mod_part.reshape(nl * 8, -1)], "gather_mod")
    mod_all = jnp.moveaxis(_chip_blocks(g_mod, 6 * D // N_CHIPS).reshape(N_CHIPS, nl, 8, -1), 0, 2)
    mod_all = mod_all.reshape(nl, 8, 6 * D) + b_ada[:, None, :]
    mod_mine = lax.dynamic_index_in_dim(mod_all, me, axis=1, keepdims=False).reshape(nl, 6, D)
    mods = jnp.pad(mod_mine, ((0, 0), (0, 2), (0, 0)))

    halves = {}
    for n in BIG:
        wt = jnp.swapaxes(W[n], 1, 2) if n in BIG_T else W[n]
        hr = wt.shape[1] // 2
        halves[n] = lax.dynamic_slice_in_dim(wt, ci * hr, hr, axis=1).astype(BF16)
    half_l = [[halves[n][l] for n in BIG] for l in range(nl)]
    wfull = [dict(zip(BIG, allgather8(half_l[0], "gather_w")))]

    small_names = [n for n in WEIGHTS if n not in BIG and n not in ('w_ada', 'b_ada')]
    pfull = {n: W[n] for n in small_names}
    pfull['sc_w'], pfull['cf_dw_w'] = sc_full, cf_full
    Q = prep_params(pfull)
    h = x[0]
    saved = []
    for l in range(nl):
        h, sv, w_next = layer_fwd(h, mods[l], wfull[l], {k: v[l] for k, v in Q.items()},
                                  half_l[l + 1] if l + 1 < nl else ())
        saved.append(sv)
        if w_next:
            wfull.append(dict(zip(BIG, w_next)))
    l8, dh = loss_fwd(h, loss_target[0])
    loss = lax.psum(l8[0, 0], ("x", "y", "c"))

    stats = [None] * nl
    gbig = {n: [None] * nl for n in BIG}

    def finish_reduce(l, pss, recv):
        for n, t in zip(BIG, pair_gather(sum_chips(pss, recv))):
            gbig[n][l] = t.reshape(2 * t.shape[1], t.shape[2])

    pending = ()
    for l in reversed(range(nl)):
        dh, big, stats[l], recv = layer_bwd(dh, mods[l], wfull[l], {k: v[l] for k, v in Q.items()}, saved[l], pending)
        if pending:
            finish_reduce(l + 1, pending, recv)
        views = [g.reshape(N_CHIPS, 2, g.shape[0] // (2 * N_CHIPS), g.shape[1]) for g in big]
        pending = pair_sum(views, pair_swap(views))
    finish_reduce(0, pending, chip_scatter(pending))
    grad_x = dh[None]

    mine = [jnp.concatenate([stats[l][k] for l in range(nl)], axis=0) for k in STAT_KINDS]
    rows = [v.shape[0] for v in mine]
    gathered = allgather8(mine, "gather_small")
    S = dict(zip(STAT_KINDS, [v.reshape(nl, r // nl, v.shape[1]) for v, r in zip(sum8(gathered, rows), rows)]))
    G = {}
    s1 = S['s1024']
    mod_rows = (0, 1, 8, 16, 17, 24)
    G['b_ada'] = jnp.concatenate([s1[:, r] for r in mod_rows], axis=1)
    G['ln1_g'], G['ln1_b'], G['ln2_g'], G['ln2_b'] = s1[:, 9], s1[:, 10], s1[:, 25], s1[:, 26]
    mst = S['mst']
    sc_g, cf_g = mst[:, 0:SC_TAPS], mst[:, MIXP_CW:MIXP_CW + CF_TAPS]
    G['sc_w'] = lax.dynamic_slice_in_dim(sc_g, chip * 64, 64, axis=2)
    G['cf_dw_w'] = lax.dynamic_slice_in_dim(cf_g, chip * 64, 64, axis=2)
    G['cf_dw_b'], G['cf_ln_g'], G['cf_ln_b'], G['pool_scale'] = mst[:, MIXP_CB], mst[:, MIXP_LG], mst[:, MIXP_LB], mst[:, MIXP_PS]
    G['pool_w'] = S['dpool'].reshape(nl, 4, 64, 64)
    G['ssm_d'], G['ssm_b_glu'] = S['vst'][:, 0], S['vst'][:, 1]
    G['b_in'] = jnp.concatenate([S['dbin'][:, 0], S['vst'][:, 2]], axis=1)
    dbc = S['dbc'].reshape(nl, SSM_G, SSM_H, 2, SSM_P)
    dcc = S['dcc'].reshape(nl, SSM_G, SSM_H, 2, SSM_P)
    G['ssm_c_re'], G['ssm_c_im'] = dcc[:, :, :, 0], -dcc[:, :, :, 1]
    da = S['da'][:, 0]
    cot = (da[:, :NST].reshape(nl, SSM_G, SSM_P), da[:, NST:].reshape(nl, SSM_G, SSM_P),
           jnp.swapaxes(dbc[:, :, :, 0], 2, 3), jnp.swapaxes(dbc[:, :, :, 1], 2, 3))
    ssm_in = ('ssm_lam_re', 'ssm_lam_im', 'ssm_log_dt', 'ssm_b_re', 'ssm_b_im')
    _, vjp = jax.vjp(_ssm_prep, *[W[n] for n in ssm_in])
    G.update(dict(zip(ssm_in, vjp(cot))))
    g1 = gathered[0].reshape(N_DEV, nl, 32, D)
    dmod_all = jnp.concatenate([g1[:, :, r] for r in mod_rows], axis=2)
    dm_cols = lax.dynamic_slice_in_dim(jnp.swapaxes(dmod_all, 0, 1), chip * (6 * D // N_CHIPS), 6 * D // N_CHIPS, axis=2)
    G['w_ada'] = ada_grad(c_all, dm_cols)
    for n in BIG:
        g = jnp.stack(gbig[n])
        G[n] = jnp.swapaxes(g, 1, 2) if n in BIG_T else g

    big_names = ('w_ada',) + BIG
    delta, new_m, new_v = {}, {}, {}
    for n in big_names:
        shp = W[n].shape
        two = (shp[0] * shp[1], shp[2])
        d_, m_, v_ = adamw(W[n].reshape(two), G[n].reshape(two), M[n].reshape(two), V[n].reshape(two), "adamw_" + n)
        delta[n], new_m[n], new_v[n] = d_.reshape(shp), m_.reshape(shp), v_.reshape(shp)
    rest = [n for n in WEIGHTS if n not in big_names]
    for n in rest:
        G[n] = G[n].reshape(W[n].shape)
    lane_view = lambda t: t.reshape(t.shape[:-2] + (-1,)) if t.shape[-1] < 64 and t.ndim == 4 else t
    ds, ms, vs_ = adamw_many(*[[lane_view(t[n]) for n in rest] for t in (W, G, M, V)])
    for n, d_, m_, v_ in zip(rest, ds, ms, vs_):
        delta[n], new_m[n], new_v[n] = d_.reshape(W[n].shape), m_.reshape(W[n].shape), v_.reshape(W[n].shape)
    return (loss, grad_x, *[G[n] for n in WEIGHTS], *[delta[n] for n in WEIGHTS],
            *[new_m[n] for n in WEIGHTS], *[new_v[n] for n in WEIGHTS])
```
